```python
import math
import jax
import jax.numpy as jnp
from jax import lax
import numpy as np

D_MODEL = 1024
BATCH = 16
SEQ = 2048
DEPTH = 2
DEC_BATCH = 32
DEC_SEQ = 8
PAST_LEN = 16384
PAGE_SIZE = 128

N_META = 16
BLOCK = 128
N_GROUPS = 4
N_HEADS = 4
HEAD_V = D_MODEL // (N_GROUPS * N_HEADS)
GROUP_W = N_HEADS * HEAD_V
MIX_W = N_GROUPS * GROUP_W
DIFF_DQK = HEAD_V // 2
FOX_DQK = HEAD_V
LIN_DK = HEAD_V // 2
GLA_RANK = 16
GLA_TAU = 16.0
D_FF = -(-8 * D_MODEL // (3 * 256)) * 256
EPS = 1e-5
NEG = -1e30
IN_SIZES = (
    2 * N_HEADS * DIFF_DQK, 2 * N_HEADS * DIFF_DQK, GROUP_W,
    N_HEADS * FOX_DQK, N_HEADS * FOX_DQK, GROUP_W, N_HEADS,
    N_HEADS * LIN_DK, N_HEADS * LIN_DK, GROUP_W, GROUP_W, GLA_RANK,
    N_HEADS * LIN_DK, N_HEADS * LIN_DK, GROUP_W, GROUP_W,
)
P_IN = sum(IN_SIZES)

kernel_name = "hybrid_headgroup_decoder_step"


def layer_norm(x, g, b):
    xf = x.astype(jnp.float32)
    mu = jnp.mean(xf, axis=-1, keepdims=True)
    var = jnp.mean(jnp.square(xf - mu), axis=-1, keepdims=True)
    return ((xf - mu) * lax.rsqrt(var + EPS) * g + b).astype(x.dtype)


def project(h, w_in_l, b_f_l, w_cg_l, b_cg_l):
    B, L = h.shape[:2]
    idx = np.cumsum(np.array(IN_SIZES))[:-1].tolist()
    (q_a, k_a, v_a, q_b, k_b, v_b, f_b, q_c, k_c, v_c, r_c, a_c,
     q_d, k_d, v_d, r_d) = jnp.split(h @ w_in_l, idx, axis=-1)
    heads = lambda a: a.reshape(B, L, N_HEADS, -1)
    logf = jax.nn.log_sigmoid((f_b + b_f_l).astype(jnp.float32))
    g_c = jax.nn.log_sigmoid((a_c @ w_cg_l + b_cg_l).astype(jnp.float32)) / GLA_TAU
    return dict(q_a=heads(q_a), k_a=heads(k_a), v_a=heads(v_a),
                q_b=heads(q_b), k_b=heads(k_b), v_b=heads(v_b), logf=logf,
                q_c=heads(q_c), k_c=heads(k_c), v_c=heads(v_c), r_c=heads(r_c), g_c=heads(g_c),
                q_d=heads(q_d), k_d=heads(k_d), v_d=heads(v_d), r_d=heads(r_d))


def attn_probs(q, k, bias, mask, scale):
    s = jnp.einsum('bqhd,bkhd->bhqk', q, k).astype(jnp.float32) * scale + bias
    return jax.nn.softmax(jnp.where(mask, s, NEG), axis=-1)


def attend_block(q_a, q_b, c_q, t, k_a, v_a, k_b, v_b, c_k, pos, key_valid, lam, slopes):
    mask = (pos[None, :] <= t[:, None]) & key_valid[None, :]
    dist = (t[:, None] - pos[None, :]).astype(jnp.float32)
    alibi = (-slopes[:, None, None] * dist[None])[None]
    p1 = attn_probs(q_a[..., :DIFF_DQK], k_a[..., :DIFF_DQK], alibi, mask, DIFF_DQK ** -0.5)
    p2 = attn_probs(q_a[..., DIFF_DQK:], k_a[..., DIFF_DQK:], alibi, mask, DIFF_DQK ** -0.5)
    o_a = jnp.einsum('bhqk,bkhd->bqhd', (p1 - lam * p2).astype(v_a.dtype), v_a)
    fox = jnp.swapaxes(c_q, 1, 2)[:, :, :, None] - jnp.swapaxes(c_k, 1, 2)[:, :, None, :]
    p_b = attn_probs(q_b, k_b, fox, mask, FOX_DQK ** -0.5)
    o_b = jnp.einsum('bhqk,bkhd->bqhd', p_b.astype(v_b.dtype), v_b)
    return o_a, o_b


def rel_cum_logf(logf):
    return logf - lax.cumsum(logf, axis=1, reverse=True)


def gla_chunk(S, q, k, v, g):
    f32 = jnp.float32
    q = q.astype(f32) * LIN_DK ** -0.5
    k = k.astype(f32)
    v = v.astype(f32)
    C = q.shape[1]
    b = jnp.cumsum(g, axis=1)
    causal = jnp.tril(jnp.ones((C, C), bool))[None, :, :, None, None]
    diff = b[:, :, None] - b[:, None, :]
    dec = jnp.where(causal, jnp.exp(jnp.where(causal, diff, 0.0)), 0.0)
    att = jnp.einsum('bthk,bshk,btshk->bhts', q, k, dec)
    o = (jnp.einsum('bthk,bhkv->bthv', q * jnp.exp(b), S)
         + jnp.einsum('bhts,bshv->bthv', att, v))
    b_last = b[:, -1:]
    S_new = (jnp.exp(b_last[:, 0])[..., None] * S
             + jnp.einsum('bshk,bshv->bhkv', k * jnp.exp(b_last - b), v))
    return S_new, o


def retention_chunk(S, q, k, v, log_gamma):
    f32 = jnp.float32
    q = q.astype(f32)
    k = k.astype(f32) * LIN_DK ** -0.5
    v = v.astype(f32)
    C = q.shape[1]
    t = jnp.arange(C, dtype=f32)
    causal = t[:, None] >= t[None, :]
    dist = jnp.where(causal, t[:, None] - t[None, :], 0.0)
    dec = jnp.where(causal[None], jnp.exp(dist[None] * log_gamma[:, None, None]), 0.0)
    att = jnp.einsum('bthk,bshk->bhts', q, k) * dec[None]
    o = (jnp.einsum('bhts,bshv->bthv', att, v)
         + jnp.einsum('bthk,bhkv->bthv', q, S)
         * jnp.exp((t + 1.0)[:, None] * log_gamma[None, :])[None, :, :, None])
    S_new = (jnp.exp(C * log_gamma)[None, :, None, None] * S
             + jnp.einsum('bshk,bshv->bhkv',
                          k * jnp.exp((C - 1.0 - t)[:, None] * log_gamma[None, :])[None, :, :, None], v))
    return S_new, o


def chunk_scan(step, S0, xs):
    B, L = xs[0].shape[:2]
    n = L // BLOCK
    to_chunks = lambda a: jnp.swapaxes(a.reshape(B, n, BLOCK, *a.shape[2:]), 0, 1)
    S, o = lax.scan(lambda S, c: step(S, *c), S0, tuple(to_chunks(a) for a in xs))
    return S, jnp.swapaxes(o, 0, 1).reshape(B, L, *o.shape[3:])


def mixer_prompt(h, valid, n_pad, wts, lam, slopes, log_gamma):
    pr = project(h, *wts)
    B, L = h.shape[:2]
    pos = jnp.arange(L)
    c = rel_cum_logf(pr['logf'])

    def block(i):
        st = i * BLOCK
        sl = lambda a: lax.dynamic_slice_in_dim(a, st, BLOCK, axis=1)
        return attend_block(sl(pr['q_a']), sl(pr['q_b']), sl(c), st + jnp.arange(BLOCK),
                            pr['k_a'], pr['v_a'], pr['k_b'], pr['v_b'], c, pos, valid, lam, slopes)

    o_a, o_b = lax.map(block, jnp.arange(L // BLOCK))
    unblock = lambda o: jnp.swapaxes(o, 0, 1).reshape(B, L, N_HEADS, HEAD_V)
    vm = valid.astype(h.dtype)[None, :, None, None]
    S0 = jnp.zeros((B, N_HEADS, LIN_DK, HEAD_V), jnp.float32)
    s_c, o_c = chunk_scan(gla_chunk, S0, (pr['q_c'], pr['k_c'] * vm, pr['v_c'], pr['g_c']))
    s_d, o_d = chunk_scan(lambda S, q, k, v: retention_chunk(S, q, k, v, log_gamma), S0,
                          (pr['q_d'], pr['k_d'] * vm, pr['v_d']))
    keep = lambda a: a[:, n_pad:]
    rows = (keep(pr['k_a']), keep(pr['v_a']), keep(pr['k_b']), keep(pr['v_b']), keep(pr['logf']),
            s_c.astype(h.dtype), s_d.astype(h.dtype))
    return (unblock(o_a), unblock(o_b), o_c, o_d, pr['r_c'], pr['r_d']), rows


def mixer_sample(h, page_table, ck_a, cv_a, ck_b, cv_b, clf_b, st_c, st_d, wts, lam, slopes, log_gamma):
    pr = project(h, *wts)
    B, Q = h.shape[:2]
    gather = lambda cache: cache[page_table].reshape(B, -1, *cache.shape[2:])
    k_a = jnp.concatenate([gather(ck_a), pr['k_a']], axis=1)
    v_a = jnp.concatenate([gather(cv_a), pr['v_a']], axis=1)
    k_b = jnp.concatenate([gather(ck_b), pr['k_b']], axis=1)
    v_b = jnp.concatenate([gather(cv_b), pr['v_b']], axis=1)
    logf = jnp.concatenate([gather(clf_b).astype(jnp.float32), pr['logf']], axis=1)
    T = k_a.shape[1]
    P = T - Q
    c = rel_cum_logf(logf)
    pos = jnp.arange(T)
    o_a, o_b = attend_block(pr['q_a'], pr['q_b'], c[:, P:], P + jnp.arange(Q), k_a, v_a, k_b, v_b,
                            c, pos, jnp.ones((T,), bool), lam, slopes)
    s_c, o_c = gla_chunk(st_c.astype(jnp.float32), pr['q_c'], pr['k_c'], pr['v_c'], pr['g_c'])
    s_d, o_d = retention_chunk(st_d.astype(jnp.float32), pr['q_d'], pr['k_d'], pr['v_d'], log_gamma)
    rows = (pr['k_a'], pr['v_a'], pr['k_b'], pr['v_b'], pr['logf'],
            s_c.astype(st_c.dtype), s_d.astype(st_d.dtype))
    return (o_a, o_b, o_c, o_d, pr['r_c'], pr['r_d']), rows


def merge_heads(o_a, o_b, o_c, o_d, r_c, r_d, g_out_l, w_out_l, lam_init):
    B, L = o_a.shape[:2]
    g = g_out_l.reshape(N_GROUPS, N_HEADS, HEAD_V).astype(jnp.float32)

    def rms(o, gg):
        o = o.astype(jnp.float32)
        return o * lax.rsqrt(jnp.mean(o * o, axis=-1, keepdims=True) + EPS) * gg

    cat = jnp.concatenate([rms(o_a, g[0]) * (1.0 - lam_init),
                           rms(o_b, g[1]),
                           jax.nn.silu(r_c.astype(jnp.float32)) * rms(o_c, g[2]),
                           jax.nn.silu(r_d.astype(jnp.float32)) * rms(o_d, g[3])], axis=2)
    return cat.reshape(B, L, MIX_W).astype(w_out_l.dtype) @ w_out_l


def post_layer(x, y_mix, alpha, g1, b1, w_fi, w_fo, g2, b2):
    x = layer_norm(alpha * x + y_mix, g1, b1)
    gate, up = jnp.split(x @ w_fi, 2, axis=-1)
    return layer_norm(alpha * x + (jax.nn.silu(gate) * up) @ w_fo, g2, b2)


def setup_inputs(seed: int = 0) -> dict:
    key = jax.random.key(seed)
    ks = jax.random.split(key, 26)
    f32 = jnp.float32
    n_pages = PAST_LEN // PAGE_SIZE
    n_pool = (DEC_BATCH * n_pages * 5) // 4
    beta = (8.0 * DEPTH) ** -0.25
    nrm = lambda k, shape, s=1.0: s * jax.random.normal(k, shape, f32)
    page_table = jax.random.permutation(ks[0], n_pool)[:DEC_BATCH * n_pages].reshape(
        DEC_BATCH, n_pages).astype(jnp.int32)
    pool = (DEPTH, n_pool, PAGE_SIZE, N_HEADS)
    return {
        "x_prompt": nrm(ks[1], (BATCH, SEQ, D_MODEL)),
        "x_sample": nrm(ks[2], (DEC_BATCH, DEC_SEQ, D_MODEL)),
        "cache_a_k": nrm(ks[3], pool + (2 * DIFF_DQK,)),
        "cache_a_v": nrm(ks[4], pool + (HEAD_V,)),
        "cache_b_k": nrm(ks[5], pool + (FOX_DQK,)),
        "cache_b_v": nrm(ks[6], pool + (HEAD_V,)),
        "cache_b_logf": jax.nn.log_sigmoid(2.0 + nrm(ks[7], pool, 0.5)),
        "state_c": nrm(ks[8], (DEPTH, DEC_BATCH, N_HEADS, LIN_DK, HEAD_V), 0.5),
        "state_d": nrm(ks[9], (DEPTH, DEC_BATCH, N_HEADS, LIN_DK, HEAD_V), 0.5),
        "page_table": page_table,
        "meta_tokens": nrm(ks[10], (N_META, D_MODEL)),
        "w_in": nrm(ks[11], (DEPTH, D_MODEL, P_IN), D_MODEL ** -0.5),
        "b_f": 2.0 + nrm(ks[12], (DEPTH, N_HEADS), 0.1),
        "w_c_gate": nrm(ks[13], (DEPTH, GLA_RANK, N_HEADS * LIN_DK), GLA_RANK ** -0.5),
        "b_c_gate": nrm(ks[14], (DEPTH, N_HEADS * LIN_DK), 0.1),
        "lam_params": nrm(ks[15], (DEPTH, 4, DIFF_DQK), 0.1),
        "g_out": 1.0 + nrm(ks[16], (DEPTH, MIX_W), 0.02),
        "w_out": nrm(ks[17], (DEPTH, MIX_W, D_MODEL), beta * MIX_W ** -0.5),
        "ln1_g": 1.0 + nrm(ks[18], (DEPTH, D_MODEL), 0.02),
        "ln1_b": nrm(ks[19], (DEPTH, D_MODEL), 0.02),
        "w_ffn_in": nrm(ks[20], (DEPTH, D_MODEL, 2 * D_FF), D_MODEL ** -0.5),
        "w_ffn_out": nrm(ks[21], (DEPTH, D_FF, D_MODEL), beta * D_FF ** -0.5),
        "ln2_g": 1.0 + nrm(ks[22], (DEPTH, D_MODEL), 0.02),
        "ln2_b": nrm(ks[23], (DEPTH, D_MODEL), 0.02),
    }


def reference(x_prompt, x_sample, cache_a_k, cache_a_v, cache_b_k, cache_b_v, cache_b_logf,
              state_c, state_d, page_table, meta_tokens, w_in, b_f, w_c_gate, b_c_gate,
              lam_params, g_out, w_out, ln1_g, ln1_b, w_ffn_in, w_ffn_out, ln2_g, ln2_b):
    f32 = jnp.float32
    alpha = (2.0 * DEPTH) ** 0.25
    slopes = 2.0 ** (-8.0 * (jnp.arange(N_HEADS, dtype=f32) + 1.0) / N_HEADS)
    log_gamma = jnp.log1p(-(2.0 ** (-5.0 - jnp.arange(N_HEADS, dtype=f32))))
    B, S, _ = x_prompt.shape
    n_pad = (-(N_META + S)) % BLOCK
    L = n_pad + N_META + S
    xp = jnp.concatenate([jnp.zeros((B, n_pad, D_MODEL), x_prompt.dtype),
                          jnp.broadcast_to(meta_tokens.astype(x_prompt.dtype), (B, N_META, D_MODEL)),
                          x_prompt], axis=1)
    valid = jnp.arange(L) >= n_pad
    xs = x_sample
    p_rows = []
    s_rows = []
    for l in range(DEPTH):
        lam_init = 0.8 - 0.6 * math.exp(-0.3 * l)
        lp = lam_params[l].astype(f32)
        lam = jnp.exp(jnp.sum(lp[0] * lp[1])) - jnp.exp(jnp.sum(lp[2] * lp[3])) + lam_init
        wts = (w_in[l], b_f[l], w_c_gate[l], b_c_gate[l])
        outs_p, rows_p = mixer_prompt(xp, valid, n_pad, wts, lam, slopes, log_gamma)
        xp = post_layer(xp, merge_heads(*outs_p, g_out[l], w_out[l], lam_init), alpha,
                        ln1_g[l], ln1_b[l], w_ffn_in[l], w_ffn_out[l], ln2_g[l], ln2_b[l])
        outs_s, rows_s = mixer_sample(xs, page_table, cache_a_k[l], cache_a_v[l], cache_b_k[l],
                                      cache_b_v[l], cache_b_logf[l], state_c[l], state_d[l],
                                      wts, lam, slopes, log_gamma)
        xs = post_layer(xs, merge_heads(*outs_s, g_out[l], w_out[l], lam_init), alpha,
                        ln1_g[l], ln1_b[l], w_ffn_in[l], w_ffn_out[l], ln2_g[l], ln2_b[l])
        p_rows.append(rows_p)
        s_rows.append(rows_s)
    y_prompt = xp[:, n_pad + N_META:]
    y_sample = xs
    (a_k_p, a_v_p, b_k_p, b_v_p, b_logf_p, state_c_p, state_d_p) = [jnp.stack(z) for z in zip(*p_rows)]
    (a_k_s, a_v_s, b_k_s, b_v_s, b_logf_s, state_c_s, state_d_s) = [jnp.stack(z) for z in zip(*s_rows)]
    return (y_prompt, y_sample, a_k_p, a_v_p, b_k_p, b_v_p, b_logf_p, state_c_p, state_d_p,
            a_k_s, a_v_s, b_k_s, b_v_s, b_logf_s, state_c_s, state_d_s)
```

```python
import functools
import math

import jax
import jax.numpy as jnp
from jax import lax
from jax.experimental import pallas as pl
from jax.experimental.pallas import tpu as pltpu

F32 = jnp.float32
BF16 = jnp.bfloat16

D_MODEL = 1024
N_META = 16
N_HEADS = 4
HEAD_V = 64
GROUP_W = 256
DIFF_DQK = 32
LIN_DK = 32
GLA_RANK = 16
GLA_TAU = 16.0
D_FF = 2816
EPS = 1e-5
NEG = -1e30
PAGE = 128
BLK = 128
LANE = 128
A_SCALE = DIFF_DQK ** -0.5
B_SCALE = HEAD_V ** -0.5
LIN_SCALE = LIN_DK ** -0.5
SLOPES = tuple(2.0 ** (-8.0 * (h + 1.0) / N_HEADS) for h in range(N_HEADS))
LOG_GAMMA = tuple(math.log1p(-(2.0 ** (-5.0 - h))) for h in range(N_HEADS))
IN_SIZES = (256, 256, 256, 256, 256, 256, 4, 128, 128, 256, 256, 16, 128, 128, 256, 256)
VMEM_LIMIT = 56 * 1024 * 1024


def _dot(a, b):
    return jnp.dot(a, b, preferred_element_type=F32)


def _dot_nt(a, b):
    return lax.dot_general(a, b, (((1,), (1,)), ((), ())), preferred_element_type=F32)


def _split3(x):
    hi = x.astype(BF16)
    r = x - hi.astype(F32)
    mid = r.astype(BF16)
    lo = (r - mid.astype(F32)).astype(BF16)
    return hi, mid, lo


def _dot_lhs3(x, m):
    hi, mid, lo = _split3(x)
    return _dot(hi, m) + _dot(mid, m) + _dot(lo, m)


def _dot_rhs3(m, x):
    hi, mid, lo = _split3(x)
    return _dot(m, hi) + _dot(m, mid) + _dot(m, lo)


def _iota(shape, dim):
    return lax.broadcasted_iota(jnp.int32, shape, dim)


def _tri_incl(n):
    return jnp.where(_iota((n, n), 0) >= _iota((n, n), 1), 1.0, 0.0).astype(BF16)


def _seg_mean_matrix(width, seg):
    same = (_iota((width, width), 0) // seg) == (_iota((width, width), 1) // seg)
    return jnp.where(same, 1.0 / seg, 0.0).astype(BF16)


def _block_diag_mask():
    r = _iota((LANE, GROUP_W), 0) // LIN_DK
    c = _iota((LANE, GROUP_W), 1) // HEAD_V
    return jnp.where(r == c, 1.0, 0.0).astype(F32)


def _log_sigmoid(x):
    return jnp.minimum(x, 0.0) - jnp.log1p(jnp.exp(-jnp.abs(x)))


def _silu(x):
    return x / (1.0 + jnp.exp(-x))


def _head_rms(o, mavg, gain):
    ms = _dot_lhs3(o * o, mavg)
    return o * lax.rsqrt(ms + EPS) * gain


def _layer_norm(x, g, b):
    mu = jnp.mean(x, axis=-1, keepdims=True)
    xc = x - mu
    var = jnp.mean(xc * xc, axis=-1, keepdims=True)
    return xc * lax.rsqrt(var + EPS) * g + b


def _lam(lp_ref, lam_init):
    lp = lp_ref[...]
    a = jnp.sum(lp[0:1, :] * lp[1:2, :], axis=1, keepdims=True)
    b = jnp.sum(lp[2:3, :] * lp[3:4, :], axis=1, keepdims=True)
    return jnp.exp(a) - jnp.exp(b) + lam_init


def _params(sem):
    return pltpu.CompilerParams(dimension_semantics=sem, vmem_limit_bytes=VMEM_LIMIT)


def _const_spec(shape):
    n = len(shape)
    return pl.BlockSpec(shape, lambda *_: (0,) * n, pipeline_mode=pl.Buffered(1))


W_AB = 1536
W_CD = 1536
W_FG = 256
W_ALL = W_AB + W_CD + W_FG


def _prep_w_in(w_in_l):
    offs = [0]
    for s in IN_SIZES:
        offs.append(offs[-1] + s)
    seg = lambda i: w_in_l[:, offs[i]:offs[i + 1]]
    pad = lambda a: jnp.pad(a, ((0, 0), (0, LANE - a.shape[1])))
    cols = [seg(i) for i in (0, 1, 2, 3, 4, 5)] + [seg(i) for i in (7, 8, 9, 10, 12, 13, 14, 15)]
    cols += [pad(seg(6)), pad(seg(11))]
    return jnp.concatenate(cols, axis=1).astype(BF16)


def _proj_kernel(x_ref, w_ref, bf_ref, wcg_ref, bcg_ref, qkv_ref, kv_ref, cd_ref, lg_ref):
    xb = x_ref[...].astype(BF16)
    ab = _dot(xb, w_ref[:, 0:W_AB])
    dt = qkv_ref.dtype
    qkv_ref[:, 0:256] = (ab[:, 0:256] * A_SCALE).astype(dt)
    qkv_ref[:, 256:768] = ab[:, 256:768].astype(dt)
    qkv_ref[:, 768:1024] = (ab[:, 768:1024] * B_SCALE).astype(dt)
    qkv_ref[:, 1024:1536] = ab[:, 1024:1536].astype(dt)
    kv_ref[:, 0:512] = ab[:, 256:768]
    kv_ref[:, 512:1024] = ab[:, 1024:1536]
    cd_ref[...] = _dot(xb, w_ref[:, W_AB:W_AB + W_CD])
    fa = _dot(xb, w_ref[:, W_AB + W_CD:W_ALL])
    lg_ref[:, 0:LANE] = _log_sigmoid(fa[:, 0:LANE] + bf_ref[...])
    a = fa[:, LANE:2 * LANE]
    a_hi = a.astype(BF16)
    a_lo = (a - a_hi.astype(F32)).astype(BF16)
    w = wcg_ref[...]
    w_hi = w.astype(BF16)
    w_lo = (w - w_hi.astype(F32)).astype(BF16)
    z = _dot(a_hi, w_hi) + _dot(a_hi, w_lo) + _dot(a_lo, w_hi) + bcg_ref[...]
    lg_ref[:, LANE:2 * LANE] = _log_sigmoid(z) / GLA_TAU


def _proj(x, w, bf, wcg, bcg, tm, qkv_dtype):
    t = x.shape[0]
    row = lambda n: pl.BlockSpec((tm, n), lambda i: (i, 0))
    return pl.pallas_call(
        _proj_kernel,
        grid=(t // tm,),
        in_specs=[row(D_MODEL), _const_spec((D_MODEL, W_ALL)), _const_spec((1, LANE)),
                  _const_spec((LANE, LANE)), _const_spec((1, LANE))],
        out_specs=[row(W_AB), row(1024), row(W_CD), row(2 * LANE)],
        out_shape=[jax.ShapeDtypeStruct((t, W_AB), qkv_dtype),
                   jax.ShapeDtypeStruct((t, 1024), F32),
                   jax.ShapeDtypeStruct((t, W_CD), F32),
                   jax.ShapeDtypeStruct((t, 2 * LANE), F32)],
        compiler_params=_params(("parallel",)),
        name="proj",
    )(x, w, bf, wcg, bcg)


def _rows_kernel(kv_ref, lg_ref, ka_ref, va_ref, kb_ref, vb_ref, lf_ref):
    kv = kv_ref[0]
    ka_ref[0] = kv[:, 0:256].T
    va_ref[0] = kv[:, 256:512].T
    kb_ref[0] = kv[:, 512:768].T
    vb_ref[0] = kv[:, 768:1024].T
    lf_ref[0] = lg_ref[0].T[0:N_HEADS, :]


def _rows(kv, lg, l_true):
    b, lp, _ = kv.shape
    out = pl.BlockSpec((1, GROUP_W, BLK), lambda i, j: (i, 0, j))
    return pl.pallas_call(
        _rows_kernel,
        grid=(b, lp // BLK),
        in_specs=[pl.BlockSpec((1, BLK, 1024), lambda i, j: (i, j, 0)),
                  pl.BlockSpec((1, BLK, LANE), lambda i, j: (i, j, 0))],
        out_specs=[out, out, out, out, pl.BlockSpec((1, N_HEADS, BLK), lambda i, j: (i, 0, j))],
        out_shape=[jax.ShapeDtypeStruct((b, GROUP_W, l_true), F32)] * 4
        + [jax.ShapeDtypeStruct((b, N_HEADS, l_true), F32)],
        compiler_params=_params(("parallel", "parallel")),
        name="rows",
    )(kv, lg)


def _softmax_step(s, v, idx, m_ref, l_ref, acc_ref):
    m_prev = m_ref[idx]
    m_new = jnp.maximum(m_prev, jnp.max(s, axis=1, keepdims=True))
    alpha = jnp.exp(m_prev - m_new)
    p = jnp.exp(s - m_new)
    l_ref[idx] = alpha * l_ref[idx] + jnp.sum(p, axis=1, keepdims=True)
    acc_ref[idx] = alpha * acc_ref[idx] + _dot(p.astype(BF16), v)
    m_ref[idx] = m_new


def _attn_kernel(qa_ref, ka_ref, va_ref, qb_ref, kb_ref, vb_ref, lf_ref, lp_ref, g_ref, o_ref,
                 fcol, frow, m_ref, l_ref, acc_ref, *, lam_init, nblk):
    qi = pl.program_id(1)

    @pl.when(qi == 0)
    def _():
        tri = _tri_incl(BLK)
        carry = jnp.zeros((1, LANE), F32)
        for i in range(nblk):
            cs = _dot_rhs3(tri, lf_ref[0, i * BLK:(i + 1) * BLK, :]) + carry
            fcol[i * BLK:(i + 1) * BLK, :] = cs
            frow[:, i * BLK:(i + 1) * BLK] = cs.T[0:8, :]
            carry = cs[BLK - 1:BLK, :]

    m_ref[...] = jnp.full(m_ref.shape, NEG, F32)
    l_ref[...] = jnp.zeros(l_ref.shape, F32)
    acc_ref[...] = jnp.zeros(acc_ref.shape, F32)

    qs = pl.multiple_of(qi * BLK, BLK)
    lane = _iota((BLK, LANE), 1)
    row = _iota((BLK, BLK), 0)
    col = _iota((BLK, BLK), 1)
    dist0 = (row - col).astype(F32)
    causal = row >= col
    qa = qa_ref[0]
    qb = qb_ref[0]
    fq = fcol[pl.ds(qs, BLK), :]
    zero = jnp.zeros((BLK, LANE), BF16)
    q1, q2, q3 = [], [], []
    for h in range(N_HEADS):
        j, half = divmod(h, 2)
        qa_j = qa[:, j * LANE:(j + 1) * LANE]
        qb_j = qb[:, j * LANE:(j + 1) * LANE]
        lo = half * HEAD_V
        q1.append(jnp.where((lane >= lo) & (lane < lo + DIFF_DQK), qa_j, zero))
        q2.append(jnp.where((lane >= lo + DIFF_DQK) & (lane < lo + HEAD_V), qa_j, zero))
        q3.append(jnp.where((lane >= lo) & (lane < lo + HEAD_V), qb_j, zero))

    def block(kj, masked):
        ks = pl.multiple_of(kj * BLK, BLK)
        delta = ((qi - kj) * BLK).astype(F32)
        for j in range(2):
            ka = ka_ref[0, pl.ds(ks, BLK), j * LANE:(j + 1) * LANE]
            va = va_ref[0, pl.ds(ks, BLK), j * LANE:(j + 1) * LANE]
            kb = kb_ref[0, pl.ds(ks, BLK), j * LANE:(j + 1) * LANE]
            vb = vb_ref[0, pl.ds(ks, BLK), j * LANE:(j + 1) * LANE]
            for half in range(2):
                h = 2 * j + half
                alibi = (dist0 + delta) * (-SLOPES[h])
                s1 = _dot_nt(q1[h], ka) + alibi
                s2 = _dot_nt(q2[h], ka) + alibi
                s3 = _dot_nt(q3[h], kb) + (fq[:, h:h + 1] - frow[h:h + 1, pl.ds(ks, BLK)])
                if masked:
                    s1 = jnp.where(causal, s1, NEG)
                    s2 = jnp.where(causal, s2, NEG)
                    s3 = jnp.where(causal, s3, NEG)
                _softmax_step(s1, va, 3 * h, m_ref, l_ref, acc_ref)
                _softmax_step(s2, va, 3 * h + 1, m_ref, l_ref, acc_ref)
                _softmax_step(s3, vb, 3 * h + 2, m_ref, l_ref, acc_ref)

    def body(kj, carry):
        block(kj, False)
        return carry

    lax.fori_loop(0, qi, body, 0)
    block(qi, True)

    lam = _lam(lp_ref, lam_init)
    mavg = _seg_mean_matrix(LANE, HEAD_V)
    g = g_ref[...]
    for j in range(2):
        oa, ob = [], []
        for half in range(2):
            h = 2 * j + half
            o1 = acc_ref[3 * h] / l_ref[3 * h]
            o2 = acc_ref[3 * h + 1] / l_ref[3 * h + 1]
            oa.append(o1 - lam * o2)
            ob.append(acc_ref[3 * h + 2] / l_ref[3 * h + 2])
        oa_j = jnp.where(lane < HEAD_V, oa[0], oa[1])
        ob_j = jnp.where(lane < HEAD_V, ob[0], ob[1])
        ga = g[:, j * LANE:(j + 1) * LANE] * (1.0 - lam_init)
        gb = g[:, GROUP_W + j * LANE:GROUP_W + (j + 1) * LANE]
        o_ref[0, :, j * LANE:(j + 1) * LANE] = _head_rms(oa_j, mavg, ga).astype(o_ref.dtype)
        o_ref[0, :, GROUP_W + j * LANE:GROUP_W + (j + 1) * LANE] = _head_rms(ob_j, mavg, gb).astype(o_ref.dtype)


def _attn_prompt(qkv, lg, lam_params_l, g_out_l, lam_init):
    b, lp, _ = qkv.shape
    nblk = lp // BLK
    qspec = lambda c: pl.BlockSpec((1, BLK, GROUP_W), lambda i, j: (i, j, c))
    kspec = lambda c: pl.BlockSpec((1, lp, GROUP_W), lambda i, j: (i, 0, c))
    return pl.pallas_call(
        functools.partial(_attn_kernel, lam_init=lam_init, nblk=nblk),
        grid=(b, nblk),
        in_specs=[qspec(0), kspec(1), kspec(2), qspec(3), kspec(4), kspec(5),
                  pl.BlockSpec((1, lp, LANE), lambda i, j: (i, 0, 0)),
                  _const_spec((4, DIFF_DQK)), _const_spec((1, 1024))],
        out_specs=pl.BlockSpec((1, BLK, 2 * GROUP_W), lambda i, j: (i, j, 0)),
        out_shape=jax.ShapeDtypeStruct((b, lp, 2 * GROUP_W), BF16),
        scratch_shapes=[pltpu.VMEM((lp, LANE), F32), pltpu.VMEM((8, lp), F32),
                        pltpu.VMEM((3 * N_HEADS, BLK, 1), F32), pltpu.VMEM((3 * N_HEADS, BLK, 1), F32),
                        pltpu.VMEM((3 * N_HEADS, BLK, LANE), F32)],
        compiler_params=_params(("parallel", "arbitrary")),
        name="attn_prompt",
    )(qkv, qkv, qkv, qkv, qkv, qkv, lg, lam_params_l, g_out_l)


def _lane_const(values, width, seg):
    lane = _iota((1, width), 1) // seg
    out = jnp.full((1, width), values[-1], F32)
    for h in range(len(values) - 2, -1, -1):
        out = jnp.where(lane == h, values[h], out)
    return out


def _state_rows(s_bd):
    hsum = s_bd[:, 0:LANE] + s_bd[:, LANE:2 * LANE]
    return (hsum + pltpu.roll(hsum, HEAD_V, 1))[:, 0:HEAD_V]


def _lin_kernel(cd_ref, g_ref, sc0_ref, sd0_ref, gout_ref, cat_ref, sc_ref, sd_ref, s_c, s_d, o_s,
                *, valid_len, n_tiles, nchunks):
    c = pl.program_id(1)
    bd = _block_diag_mask()
    e2 = bd.astype(BF16)

    @pl.when(c == 0)
    def _():
        expand = jnp.where(_iota((HEAD_V, GROUP_W), 0) == _iota((HEAD_V, GROUP_W), 1) % HEAD_V,
                           1.0, 0.0).astype(BF16)
        s_c[...] = _dot_lhs3(sc0_ref[0], expand) * bd
        s_d[...] = _dot_lhs3(sd0_ref[0], expand) * bd

    rowi = _iota((BLK, 1), 0)
    valid = (c * BLK + rowi) < valid_len
    n_v = jnp.clip(valid_len - c * BLK, 0, BLK).astype(F32)
    x = cd_ref[0]
    g = jnp.where(valid, g_ref[0], 0.0)
    q_c = x[:, 0:128] * LIN_SCALE
    k_c = jnp.where(valid, x[:, 128:256], 0.0)
    v_c = x[:, 256:512]
    r_c = x[:, 512:768]
    q_d = x[:, 768:896]
    k_d = jnp.where(valid, x[:, 896:1024], 0.0) * LIN_SCALE
    v_d = x[:, 1024:1280]
    r_d = x[:, 1280:1536]

    b = _dot_rhs3(_tri_incl(BLK), g)
    b_last = b[BLK - 1:BLK, :]
    s_prev = s_c[...]
    o_s[...] = _dot((q_c * jnp.exp(b)).astype(BF16), s_prev.astype(BF16))
    for ti in range(n_tiles):
        t0 = 8 * ti
        ns = t0 + 8
        ws = []
        for r in range(8):
            t = t0 + r
            diff = b[t:t + 1, :] - b[0:ns, :]
            dec = jnp.exp(jnp.where(rowi[0:ns] <= t, diff, NEG))
            ws.append(dec * (k_c[0:ns] * q_c[t:t + 1, :]))
        att = _dot(jnp.concatenate(ws, axis=0).astype(BF16), e2)
        o_t = jnp.sum(att.reshape(8, ns, GROUP_W) * v_c[0:ns][None], axis=1)
        o_s[t0:t0 + 8, :] = o_s[t0:t0 + 8, :] + o_t
    o_c = o_s[...]
    kt = k_c * jnp.exp(b_last - b)
    d_col = jnp.broadcast_to(jnp.exp(b_last), (LANE, LANE)).T
    s_new = s_prev * jnp.concatenate([d_col, d_col], axis=1) + _dot(kt.T.astype(BF16), v_c.astype(BF16)) * bd
    s_c[...] = s_new

    lgl = _lane_const(LOG_GAMMA, LANE, LIN_DK)
    rowf = rowi.astype(F32)
    d_prev = s_d[...]
    o_d = _dot((q_d * jnp.exp((rowf + 1.0) * lgl)).astype(BF16), d_prev.astype(BF16))
    lane_h = _iota((BLK, LANE), 1) // LIN_DK
    col_h = _iota((BLK, GROUP_W), 1) // HEAD_V
    rr = _iota((BLK, BLK), 0)
    cc = _iota((BLK, BLK), 1)
    dist = (rr - cc).astype(F32)
    kdb = k_d.astype(BF16)
    vdb = v_d.astype(BF16)
    for h in range(N_HEADS):
        qm = jnp.where(lane_h == h, q_d, 0.0).astype(BF16)
        dec = jnp.exp(jnp.where(rr >= cc, dist * LOG_GAMMA[h], NEG))
        att = (_dot_nt(qm, kdb) * dec).astype(BF16)
        o_d = o_d + jnp.where(col_h == h, _dot(att, vdb), 0.0)
    deck = jnp.where(valid, jnp.exp((n_v - 1.0 - rowf) * lgl), 0.0)
    lgr = jnp.full((LANE, 1), LOG_GAMMA[-1], F32)
    rrow = _iota((LANE, 1), 0) // LIN_DK
    for h in range(N_HEADS - 2, -1, -1):
        lgr = jnp.where(rrow == h, LOG_GAMMA[h], lgr)
    d_new = d_prev * jnp.exp(n_v * lgr) + _dot((k_d * deck).T.astype(BF16), vdb) * bd
    s_d[...] = d_new

    mavg = _seg_mean_matrix(GROUP_W, HEAD_V)
    gout = gout_ref[...]
    cat_ref[0, :, 0:GROUP_W] = (_silu(r_c) * _head_rms(o_c, mavg, gout[:, 512:768])).astype(cat_ref.dtype)
    cat_ref[0, :, GROUP_W:2 * GROUP_W] = (_silu(r_d) * _head_rms(o_d, mavg, gout[:, 768:1024])).astype(cat_ref.dtype)

    @pl.when(c == nchunks - 1)
    def _():
        sc_ref[0] = _state_rows(s_new)
        sd_ref[0] = _state_rows(d_new)


def _lin(cd, lg, sc0, sd0, g_out_l, valid_len, cat_dtype):
    b, lp, _ = cd.shape
    nchunks = lp // BLK
    n_tiles = BLK // 8 if nchunks > 1 else -(-valid_len // 8)
    st = pl.BlockSpec((1, LANE, HEAD_V), lambda i, j: (i, 0, 0))
    return pl.pallas_call(
        functools.partial(_lin_kernel, valid_len=valid_len, n_tiles=n_tiles, nchunks=nchunks),
        grid=(b, nchunks),
        in_specs=[pl.BlockSpec((1, BLK, W_CD), lambda i, j: (i, j, 0)),
                  pl.BlockSpec((1, BLK, LANE), lambda i, j: (i, j, 1)),
                  st, st, _const_spec((1, 1024))],
        out_specs=[pl.BlockSpec((1, BLK, 2 * GROUP_W), lambda i, j: (i, j, 0)), st, st],
        out_shape=[jax.ShapeDtypeStruct((b, lp, 2 * GROUP_W), cat_dtype),
                   jax.ShapeDtypeStruct((b, LANE, HEAD_V), F32),
                   jax.ShapeDtypeStruct((b, LANE, HEAD_V), F32)],
        scratch_shapes=[pltpu.VMEM((LANE, GROUP_W), F32), pltpu.VMEM((LANE, GROUP_W), F32),
                        pltpu.VMEM((BLK, GROUP_W), F32)],
        compiler_params=_params(("parallel", "arbitrary")),
        name="lin",
    )(cd, lg, sc0, sd0, g_out_l)


FF_CHUNK = 256


def _post_kernel(x_ref, cab_ref, ccd_ref, wo_ref, g1_ref, b1_ref, wfi_ref, wfo_ref, g2_ref, b2_ref, o_ref,
                 act_ref, *, alpha):
    y = (_dot(cab_ref[...].astype(BF16), wo_ref[0:512, :])
         + _dot(ccd_ref[...].astype(BF16), wo_ref[512:1024, :]))
    x1 = _layer_norm(alpha * x_ref[...] + y, g1_ref[...], b1_ref[...])
    x1b = x1.astype(BF16)
    for j in range(D_FF // FF_CHUNK):
        gate = _dot(x1b, wfi_ref[:, j * FF_CHUNK:(j + 1) * FF_CHUNK])
        up = _dot(x1b, wfi_ref[:, D_FF + j * FF_CHUNK:D_FF + (j + 1) * FF_CHUNK])
        act_ref[:, j * FF_CHUNK:(j + 1) * FF_CHUNK] = (_silu(gate) * up).astype(BF16)
    y2 = _dot(act_ref[...], wfo_ref[...])
    o_ref[...] = _layer_norm(alpha * x1 + y2, g2_ref[...], b2_ref[...])


def _post(x, cab, ccd, wo, g1, b1, wfi, wfo, g2, b2, tm, alpha):
    t = x.shape[0]
    row = lambda n: pl.BlockSpec((tm, n), lambda i: (i, 0))
    vec = _const_spec((1, D_MODEL))
    return pl.pallas_call(
        functools.partial(_post_kernel, alpha=alpha),
        grid=(t // tm,),
        in_specs=[row(D_MODEL), row(512), row(512), _const_spec((D_MODEL, D_MODEL)), vec, vec,
                  _const_spec((D_MODEL, 2 * D_FF)), _const_spec((D_FF, D_MODEL)), vec, vec],
        out_specs=row(D_MODEL),
        out_shape=jax.ShapeDtypeStruct((t, D_MODEL), F32),
        scratch_shapes=[pltpu.VMEM((tm, D_FF), BF16)],
        compiler_params=_params(("parallel",)),
        name="post",
    )(x, cab, ccd, wo, g1, b1, wfi, wfo, g2, b2)


PAGES_PER_STEP = 4
NQ = 8


def _rep_rows(x4):
    return jnp.concatenate([jnp.broadcast_to(x4[h:h + 1, :], (NQ, LANE)) for h in range(N_HEADS)], axis=0)


def _dec_kernel(pt_ref, *refs, lam_init, past_len, n_steps):
    g_pages = PAGES_PER_STEP
    ka_refs = refs[0:g_pages]
    va_refs = refs[g_pages:2 * g_pages]
    kb_refs = refs[2 * g_pages:3 * g_pages]
    vb_refs = refs[3 * g_pages:4 * g_pages]
    lf_refs = refs[4 * g_pages:5 * g_pages]
    (q_ref, kvn_ref, lfr_ref, lfc_ref, lp_ref, g_ref, o_ref,
     qa_s, qb_s, m_s, l_s, acc_s, carry_s, nq_s) = refs[5 * g_pages:]
    step = pl.program_id(1)
    n_pages = n_steps * g_pages

    rows_a = 2 * N_HEADS * NQ
    rows_b = N_HEADS * NQ
    ra = _iota((rows_a, 1), 0)
    slope_a = jnp.full((rows_a, 1), SLOPES[-1], F32)
    for h in range(N_HEADS - 2, -1, -1):
        slope_a = jnp.where(ra // (2 * NQ) == h, SLOPES[h], slope_a)
    t_a = (past_len + ra % NQ).astype(F32)

    @pl.when(step == 0)
    def _():
        q = q_ref[...]
        qa = jnp.concatenate([q[:, 0:GROUP_W]] * (2 * N_HEADS), axis=0)
        r = _iota((rows_a, GROUP_W), 0)
        cidx = _iota((rows_a, GROUP_W), 1)
        keep = (cidx // HEAD_V == r // (2 * NQ)) & ((cidx % HEAD_V) // DIFF_DQK == (r // NQ) % 2)
        qa_s[...] = jnp.where(keep, qa, 0.0).astype(BF16)
        qb = jnp.concatenate([q[:, 768:1024]] * N_HEADS, axis=0)
        r = _iota((rows_b, GROUP_W), 0)
        cidx = _iota((rows_b, GROUP_W), 1)
        qb_s[...] = jnp.where(cidx // HEAD_V == r // NQ, qb, 0.0).astype(BF16)
        m_s[...] = jnp.full(m_s.shape, NEG, F32)
        l_s[...] = jnp.zeros(l_s.shape, F32)
        acc_s[...] = jnp.zeros(acc_s.shape, F32)
        carry_s[...] = jnp.zeros(carry_s.shape, F32)
        cum = _dot_rhs3(_tri_incl(NQ), lfc_ref[0])
        nq_s[...] = jnp.concatenate(
            [jnp.broadcast_to(cum[:, h:h + 1], (NQ, LANE)) for h in range(N_HEADS)], axis=0)

    def update(s_a, s_b, va_t, vb_t, nt):
        s = jnp.concatenate([s_a, s_b], axis=0)
        m_prev = m_s[...]
        m_new = jnp.maximum(m_prev, jnp.max(s, axis=1, keepdims=True))
        alpha = jnp.exp(m_prev - m_new)
        p = jnp.exp(s - m_new)
        l_s[...] = alpha * l_s[...] + jnp.sum(p, axis=1, keepdims=True)
        pb = p.astype(BF16)
        if nt:
            pv_a = _dot_nt(pb[0:rows_a], va_t)
            pv_b = _dot_nt(pb[rows_a:], vb_t)
        else:
            pv_a = _dot(pb[0:rows_a], va_t)
            pv_b = _dot(pb[rows_a:], vb_t)
        acc_s[...] = alpha * acc_s[...] + jnp.concatenate([pv_a, pv_b], axis=0)
        m_s[...] = m_new

    lane_f = _iota((1, LANE), 1).astype(F32)
    strict = jnp.where(_iota((LANE, LANE), 0) > _iota((LANE, LANE), 1), 1.0, 0.0).astype(BF16)
    for r in range(g_pages):
        page = n_pages - 1 - (step * g_pages + r)
        pos = (page * PAGE).astype(F32) + lane_f
        kt_a = ka_refs[r][0, 0].reshape(GROUP_W, PAGE).astype(BF16)
        kt_b = kb_refs[r][0, 0].reshape(GROUP_W, PAGE).astype(BF16)
        vt_a = va_refs[r][0, 0].reshape(GROUP_W, PAGE).astype(BF16)
        vt_b = vb_refs[r][0, 0].reshape(GROUP_W, PAGE).astype(BF16)
        lf = jnp.concatenate([lf_refs[r][0, 0], jnp.zeros((8 - N_HEADS, PAGE), F32)], axis=0)
        carry = carry_s[...]
        csuf = _dot_lhs3(lf, strict) + carry
        carry_s[...] = carry + jnp.sum(lf, axis=1, keepdims=True)
        s_a = _dot(qa_s[...], kt_a) - slope_a * (t_a - pos)
        s_b = _dot(qb_s[...], kt_b) + nq_s[...] + _rep_rows(csuf)
        update(s_a, s_b, vt_a, vt_b, True)

    @pl.when(step == n_steps - 1)
    def _():
        kvn = kvn_ref[...]
        zpad = jnp.zeros((PAGE - NQ, GROUP_W), F32)
        kn_a = jnp.concatenate([kvn[:, 0:256], zpad], axis=0).astype(BF16)
        vn_a = jnp.concatenate([kvn[:, 256:512], zpad], axis=0).astype(BF16)
        kn_b = jnp.concatenate([kvn[:, 512:768], zpad], axis=0).astype(BF16)
        vn_b = jnp.concatenate([kvn[:, 768:1024], zpad], axis=0).astype(BF16)
        jl = _iota((1, LANE), 1)
        ok_a = jl <= ra % NQ
        rb = _iota((rows_b, 1), 0)
        ok_b = jl <= rb % NQ
        cum_row = _dot_lhs3(lfr_ref[0], jnp.where(_iota((LANE, LANE), 0) <= _iota((LANE, LANE), 1),
                                                  1.0, 0.0).astype(BF16))
        s_a = _dot_nt(qa_s[...], kn_a) - slope_a * ((ra % NQ).astype(F32) - jl.astype(F32))
        s_b = _dot_nt(qb_s[...], kn_b) + nq_s[...] - _rep_rows(cum_row)
        update(jnp.where(ok_a, s_a, NEG), jnp.where(ok_b, s_b, NEG), vn_a, vn_b, False)

        lam = _lam(lp_ref, lam_init)
        acc = acc_s[...] / l_s[...]
        col_h = _iota((NQ, GROUP_W), 1) // HEAD_V
        o_a = jnp.zeros((NQ, GROUP_W), F32)
        o_b = jnp.zeros((NQ, GROUP_W), F32)
        for h in range(N_HEADS):
            r0 = 2 * NQ * h
            o_a = o_a + jnp.where(col_h == h, acc[r0:r0 + NQ] - lam * acc[r0 + NQ:r0 + 2 * NQ], 0.0)
            rb0 = rows_a + NQ * h
            o_b = o_b + jnp.where(col_h == h, acc[rb0:rb0 + NQ], 0.0)
        mavg = _seg_mean_matrix(GROUP_W, HEAD_V)
        g = g_ref[...]
        o_ref[:, 0:GROUP_W] = _head_rms(o_a, mavg, g[:, 0:256] * (1.0 - lam_init))
        o_ref[:, GROUP_W:2 * GROUP_W] = _head_rms(o_b, mavg, g[:, 256:512])


def _attn_sample(page_table, caches, layer, qkv, kvn, lf_row, lf_col, lam_params_l, g_out_l, lam_init):
    ck_a, cv_a, ck_b, cv_b, c_lf = caches
    nb, n_pages = page_table.shape
    g_pages = PAGES_PER_STEP
    n_steps = n_pages // g_pages
    past_len = n_pages * PAGE

    def page_spec(r, lf=False):
        def imap(i, s, pt):
            return (layer, pt[i, n_pages - 1 - (s * g_pages + r)], 0, 0) + (() if lf else (0,))
        return pl.BlockSpec((1, 1, N_HEADS, PAGE) if lf else (1, 1, N_HEADS, HEAD_V, PAGE), imap)

    tok = lambda n: pl.BlockSpec((NQ, n), lambda i, s, pt: (i, 0))
    small = lambda: pl.BlockSpec((1, 8, LANE), lambda i, s, pt: (i, 0, 0))
    in_specs = ([page_spec(r) for r in range(g_pages)] * 4 + [page_spec(r, True) for r in range(g_pages)]
                + [tok(W_AB), tok(1024), small(), small(),
                   pl.BlockSpec((4, DIFF_DQK), lambda i, s, pt: (0, 0)),
                   pl.BlockSpec((1, 1024), lambda i, s, pt: (0, 0))])
    rows = 3 * N_HEADS * NQ
    return pl.pallas_call(
        functools.partial(_dec_kernel, lam_init=lam_init, past_len=past_len, n_steps=n_steps),
        grid_spec=pltpu.PrefetchScalarGridSpec(
            num_scalar_prefetch=1,
            grid=(nb, n_steps),
            in_specs=in_specs,
            out_specs=pl.BlockSpec((NQ, 2 * GROUP_W), lambda i, s, pt: (i, 0)),
            scratch_shapes=[pltpu.VMEM((2 * N_HEADS * NQ, GROUP_W), BF16),
                            pltpu.VMEM((N_HEADS * NQ, GROUP_W), BF16),
                            pltpu.VMEM((rows, 1), F32), pltpu.VMEM((rows, 1), F32),
                            pltpu.VMEM((rows, GROUP_W), F32),
                            pltpu.VMEM((8, 1), F32), pltpu.VMEM((N_HEADS * NQ, LANE), F32)]),
        out_shape=jax.ShapeDtypeStruct((nb * NQ, 2 * GROUP_W), F32),
        compiler_params=_params(("parallel", "arbitrary")),
        name="attn_sample",
    )(page_table, *([ck_a] * g_pages + [cv_a] * g_pages + [ck_b] * g_pages + [cv_b] * g_pages
                    + [c_lf] * g_pages), qkv, kvn, lf_row, lf_col, lam_params_l, g_out_l)


def kernel(x_prompt, x_sample, cache_a_k, cache_a_v, cache_b_k, cache_b_v, cache_b_logf, state_c, state_d,
           page_table, meta_tokens, w_in, b_f, w_c_gate, b_c_gate, lam_params, g_out, w_out, ln1_g, ln1_b,
           w_ffn_in, w_ffn_out, ln2_g, ln2_b):
    depth = w_in.shape[0]
    alpha = (2.0 * depth) ** 0.25
    nb, seq, _ = x_prompt.shape
    db, nq, _ = x_sample.shape
    assert nq == NQ
    l_true = N_META + seq
    lp = -(-l_true // BLK) * BLK
    xp = jnp.concatenate([jnp.broadcast_to(meta_tokens.astype(F32), (nb, N_META, D_MODEL)), x_prompt,
                          jnp.zeros((nb, lp - l_true, D_MODEL), F32)], axis=1).reshape(nb * lp, D_MODEL)
    xs = x_sample.reshape(db * nq, D_MODEL)
    tm_p = 512 if (nb * lp) % 512 == 0 else BLK
    tm_s = db * nq

    tr = lambda c: jnp.transpose(c, (0, 1, 3, 4, 2))
    caches = (tr(cache_a_k), tr(cache_a_v), tr(cache_b_k), tr(cache_b_v), jnp.transpose(cache_b_logf, (0, 1, 3, 2)))

    p_rows, s_rows = [], []
    for l in range(depth):
        lam_init = 0.8 - 0.6 * math.exp(-0.3 * l)
        w = _prep_w_in(w_in[l])
        bf = jnp.pad(b_f[l], (0, LANE - N_HEADS)).reshape(1, LANE)
        wcg = jnp.pad(w_c_gate[l], ((0, LANE - GLA_RANK), (0, 0)))
        bcg = b_c_gate[l].reshape(1, LANE)
        gl = g_out[l].reshape(1, 1024)
        wo = w_out[l].astype(BF16)
        wfi = w_ffn_in[l].astype(BF16)
        wfo = w_ffn_out[l].astype(BF16)
        vec = lambda a: a[l].reshape(1, D_MODEL)
        post = functools.partial(_post, wo=wo, g1=vec(ln1_g), b1=vec(ln1_b), wfi=wfi, wfo=wfo,
                                 g2=vec(ln2_g), b2=vec(ln2_b), alpha=alpha)

        qkv, kv, cd, lg = _proj(xp, w, bf, wcg, bcg, tm_p, BF16)
        kv3 = kv.reshape(nb, lp, 1024)
        lg3 = lg.reshape(nb, lp, 2 * LANE)
        ka_t, va_t, kb_t, vb_t, lf_t = _rows(kv3, lg3, l_true)
        cab = _attn_prompt(qkv.reshape(nb, lp, W_AB), lg3, lam_params[l], gl, lam_init)
        zero_state = jnp.zeros((nb, LANE, HEAD_V), F32)
        ccd, sc_p, sd_p = _lin(cd.reshape(nb, lp, W_CD), lg3, zero_state, zero_state, gl, l_true, BF16)
        xp = post(xp, cab.reshape(nb * lp, 512), ccd.reshape(nb * lp, 512), tm=tm_p)
        heads_t = lambda a: jnp.transpose(a.reshape(nb, N_HEADS, HEAD_V, l_true), (0, 3, 1, 2))
        p_rows.append((heads_t(ka_t), heads_t(va_t), heads_t(kb_t), heads_t(vb_t),
                       jnp.transpose(lf_t, (0, 2, 1)),
                       sc_p.reshape(nb, N_HEADS, LIN_DK, HEAD_V), sd_p.reshape(nb, N_HEADS, LIN_DK, HEAD_V)))

        qkv_s, kv_s, cd_s, lg_s = _proj(xs, w, bf, wcg, bcg, tm_s, F32)
        lf_s = lg_s[:, 0:N_HEADS].reshape(db, nq, N_HEADS)
        lf_col = jnp.pad(lf_s, ((0, 0), (0, 0), (0, LANE - N_HEADS)))
        lf_row = jnp.pad(jnp.transpose(lf_s, (0, 2, 1)), ((0, 0), (0, 8 - N_HEADS), (0, LANE - nq)))
        cab_s = _attn_sample(page_table, caches, l, qkv_s, kv_s, lf_row, lf_col, lam_params[l], gl, lam_init)
        pad_rows = lambda a: jnp.pad(a.reshape(db, nq, a.shape[-1]), ((0, 0), (0, BLK - nq), (0, 0)))
        ccd_s, sc_s, sd_s = _lin(pad_rows(cd_s), pad_rows(lg_s), state_c[l].reshape(db, LANE, HEAD_V),
                                 state_d[l].reshape(db, LANE, HEAD_V), gl, nq, F32)
        xs = post(xs, cab_s, ccd_s[:, 0:nq].reshape(db * nq, 512), tm=tm_s)
        heads_s = lambda a: a.reshape(db, nq, N_HEADS, HEAD_V)
        s_rows.append((heads_s(kv_s[:, 0:256]), heads_s(kv_s[:, 256:512]), heads_s(kv_s[:, 512:768]),
                       heads_s(kv_s[:, 768:1024]), lf_s,
                       sc_s.reshape(db, N_HEADS, LIN_DK, HEAD_V), sd_s.reshape(db, N_HEADS, LIN_DK, HEAD_V)))

    y_prompt = xp.reshape(nb, lp, D_MODEL)[:, N_META:l_true]
    y_sample = xs.reshape(db, nq, D_MODEL)
    p_out = [jnp.stack(z) for z in zip(*p_rows)]
    s_out = [jnp.stack(z) for z in zip(*s_rows)]
    return (y_prompt, y_sample, *p_out, *s_out)
```

```python
import functools
import math

import jax
import jax.numpy as jnp
from jax import lax
from jax.experimental import pallas as pl
from jax.experimental.pallas import tpu as pltpu

F32 = jnp.float32
BF16 = jnp.bfloat16

D_MODEL = 1024
N_META = 16
N_HEADS = 4
HEAD_V = 64
GROUP_W = 256
DIFF_DQK = 32
LIN_DK = 32
GLA_RANK = 16
GLA_TAU = 16.0
D_FF = 2816
EPS = 1e-5
NEG = -1e30
PAGE = 128
BLK = 128
LANE = 128
A_SCALE = DIFF_DQK ** -0.5
B_SCALE = HEAD_V ** -0.5
LIN_SCALE = LIN_DK ** -0.5
SLOPES = tuple(2.0 ** (-8.0 * (h + 1.0) / N_HEADS) for h in range(N_HEADS))
LOG_GAMMA = tuple(math.log1p(-(2.0 ** (-5.0 - h))) for h in range(N_HEADS))
LOG2E = math.log2(math.e)
IN_SIZES = (256, 256, 256, 256, 256, 256, 4, 128, 128, 256, 256, 16, 128, 128, 256, 256)
VMEM_LIMIT = 56 * 1024 * 1024


def _dot(a, b):
    return jnp.dot(a, b, preferred_element_type=F32)


def _dot_nt(a, b):
    return lax.dot_general(a, b, (((1,), (1,)), ((), ())), preferred_element_type=F32)


def _split3(x):
    hi = x.astype(BF16)
    r = x - hi.astype(F32)
    mid = r.astype(BF16)
    lo = (r - mid.astype(F32)).astype(BF16)
    return hi, mid, lo


def _dot_lhs3(x, m):
    hi, mid, lo = _split3(x)
    return _dot(hi, m) + _dot(mid, m) + _dot(lo, m)


def _dot_rhs3(m, x):
    hi, mid, lo = _split3(x)
    return _dot(m, hi) + _dot(m, mid) + _dot(m, lo)


def _iota(shape, dim):
    return lax.broadcasted_iota(jnp.int32, shape, dim)


def _tri_incl(n):
    return jnp.where(_iota((n, n), 0) >= _iota((n, n), 1), 1.0, 0.0).astype(BF16)


def _seg_mean_matrix(width, seg):
    same = (_iota((width, width), 0) // seg) == (_iota((width, width), 1) // seg)
    return jnp.where(same, 1.0 / seg, 0.0).astype(BF16)


def _block_diag_mask():
    r = _iota((LANE, GROUP_W), 0) // LIN_DK
    c = _iota((LANE, GROUP_W), 1) // HEAD_V
    return jnp.where(r == c, 1.0, 0.0).astype(F32)


def _log_sigmoid(x):
    return jnp.minimum(x, 0.0) - jnp.log1p(jnp.exp(-jnp.abs(x)))


def _silu(x):
    return x / (1.0 + jnp.exp(-x))


def _head_rms(o, mavg, gain):
    ms = _dot_lhs3(o * o, mavg)
    return o * lax.rsqrt(ms + EPS) * gain


def _layer_norm(x, g, b):
    mu = jnp.mean(x, axis=-1, keepdims=True)
    xc = x - mu
    var = jnp.mean(xc * xc, axis=-1, keepdims=True)
    return xc * lax.rsqrt(var + EPS) * g + b


def _lam(lp_ref, lam_init):
    lp = lp_ref[...]
    a = jnp.sum(lp[0:1, :] * lp[1:2, :], axis=1, keepdims=True)
    b = jnp.sum(lp[2:3, :] * lp[3:4, :], axis=1, keepdims=True)
    return jnp.exp(a) - jnp.exp(b) + lam_init


def _params(sem):
    return pltpu.CompilerParams(dimension_semantics=sem, vmem_limit_bytes=VMEM_LIMIT)


def _const_spec(shape):
    n = len(shape)
    return pl.BlockSpec(shape, lambda *_: (0,) * n, pipeline_mode=pl.Buffered(1))


W_AB = 1536
W_CD = 1536
W_FG = 256
W_ALL = W_AB + W_CD + W_FG


def _prep_w_in(w_in_l):
    offs = [0]
    for s in IN_SIZES:
        offs.append(offs[-1] + s)
    seg = lambda i: w_in_l[:, offs[i]:offs[i + 1]]
    pad = lambda a: jnp.pad(a, ((0, 0), (0, LANE - a.shape[1])))
    cols = [seg(i) for i in (0, 1, 2, 3, 4, 5)] + [seg(i) for i in (7, 8, 9, 10, 12, 13, 14, 15)]
    cols += [pad(seg(6)), pad(seg(11))]
    return jnp.concatenate(cols, axis=1).astype(BF16)


def _proj_kernel(x_ref, w_ref, bf_ref, wcg_ref, bcg_ref, qkv_ref, kv_ref, cd_ref, lg_ref, vt_ref, *, q_scale):
    xb = x_ref[...].astype(BF16)
    ab = _dot(xb, w_ref[:, 0:W_AB])
    dt = qkv_ref.dtype
    qkv_ref[:, 0:256] = (ab[:, 0:256] * (A_SCALE * q_scale)).astype(dt)
    qkv_ref[:, 256:768] = ab[:, 256:768].astype(dt)
    qkv_ref[:, 768:1024] = (ab[:, 768:1024] * (B_SCALE * q_scale)).astype(dt)
    qkv_ref[:, 1024:1536] = ab[:, 1024:1536].astype(dt)
    kv_ref[:, 0:512] = ab[:, 256:768]
    kv_ref[:, 512:1024] = ab[:, 1024:1536]
    vt_ref[0:GROUP_W, :] = ab[:, 512:768].T.astype(BF16)
    vt_ref[GROUP_W:2 * GROUP_W, :] = ab[:, 1280:1536].T.astype(BF16)
    cd_ref[...] = _dot(xb, w_ref[:, W_AB:W_AB + W_CD])
    fa = _dot(xb, w_ref[:, W_AB + W_CD:W_ALL])
    lg_ref[:, 0:LANE] = _log_sigmoid(fa[:, 0:LANE] + bf_ref[...])
    a = fa[:, LANE:2 * LANE]
    a_hi = a.astype(BF16)
    a_lo = (a - a_hi.astype(F32)).astype(BF16)
    w = wcg_ref[...]
    w_hi = w.astype(BF16)
    w_lo = (w - w_hi.astype(F32)).astype(BF16)
    z = _dot(a_hi, w_hi) + _dot(a_hi, w_lo) + _dot(a_lo, w_hi) + bcg_ref[...]
    lg_ref[:, LANE:2 * LANE] = _log_sigmoid(z) / GLA_TAU


def _proj(x, w, bf, wcg, bcg, tm, qkv_dtype, q_scale):
    t = x.shape[0]
    row = lambda n: pl.BlockSpec((tm, n), lambda i: (i, 0))
    return pl.pallas_call(
        functools.partial(_proj_kernel, q_scale=q_scale),
        grid=(t // tm,),
        in_specs=[row(D_MODEL), _const_spec((D_MODEL, W_ALL)), _const_spec((1, LANE)),
                  _const_spec((LANE, LANE)), _const_spec((1, LANE))],
        out_specs=[row(W_AB), row(1024), row(W_CD), row(2 * LANE),
                   pl.BlockSpec((2 * GROUP_W, tm), lambda i: (0, i))],
        out_shape=[jax.ShapeDtypeStruct((t, W_AB), qkv_dtype),
                   jax.ShapeDtypeStruct((t, 1024), F32),
                   jax.ShapeDtypeStruct((t, W_CD), F32),
                   jax.ShapeDtypeStruct((t, 2 * LANE), F32),
                   jax.ShapeDtypeStruct((2 * GROUP_W, t), BF16)],
        compiler_params=_params(("parallel",)),
        name="proj",
    )(x, w, bf, wcg, bcg)


def _rows_kernel(*refs, depth):
    kv_refs = refs[0:depth]
    lg_refs = refs[depth:2 * depth]
    ka_ref, va_ref, kb_ref, vb_ref, lf_ref = refs[2 * depth:]
    for l in range(depth):
        kv = kv_refs[l][0]
        ka_ref[l, 0] = kv[:, 0:256].T
        va_ref[l, 0] = kv[:, 256:512].T
        kb_ref[l, 0] = kv[:, 512:768].T
        vb_ref[l, 0] = kv[:, 768:1024].T
        lf_ref[l, 0] = lg_refs[l][0].T[0:N_HEADS, :]


def _rows(kvs, lgs, l_true):
    depth = len(kvs)
    b, lp, _ = kvs[0].shape
    out = pl.BlockSpec((depth, 1, GROUP_W, BLK), lambda i, j: (0, i, 0, j))
    return pl.pallas_call(
        functools.partial(_rows_kernel, depth=depth),
        grid=(b, lp // BLK),
        in_specs=[pl.BlockSpec((1, BLK, 1024), lambda i, j: (i, j, 0))] * depth
        + [pl.BlockSpec((1, BLK, LANE), lambda i, j: (i, j, 0))] * depth,
        out_specs=[out, out, out, out, pl.BlockSpec((depth, 1, N_HEADS, BLK), lambda i, j: (0, i, 0, j))],
        out_shape=[jax.ShapeDtypeStruct((depth, b, GROUP_W, l_true), F32)] * 4
        + [jax.ShapeDtypeStruct((depth, b, N_HEADS, l_true), F32)],
        compiler_params=_params(("parallel", "parallel")),
        name="rows",
    )(*kvs, *lgs)


BIAS_F0 = 3 * N_HEADS


A_W = 4 * BLK
B_W = 2 * BLK
S_W = 2 * A_W + 2 * B_W


def _map_lanes(h, m):
    j, half = divmod(h, 2)
    if m < 2:
        return j * A_W + (2 * half + m) * BLK
    return 2 * A_W + j * B_W + half * BLK


def _attn_kernel(qa_ref, ka_ref, qb_ref, kb_ref, vt_ref, lf_ref, lp_ref, g_ref, o_ref,
                 kbias, wqa, wqb, m_ref, l_ref, acc_ref, *, lam_init, nblk):
    qi = pl.program_id(1)
    row = _iota((BLK, BLK), 0)
    col = _iota((BLK, BLK), 1)

    @pl.when(qi == 0)
    def _():
        tri = _tri_incl(BLK)
        sel = [[jnp.where((row < N_HEADS) & (col == base + 3 * row + i), 1.0, 0.0).astype(BF16)
                for i in range(3)] for base in (0, BIAS_F0)]
        slope = _lane_const(tuple(LOG2E * s for s in SLOPES) + (0.0,), LANE, 1)
        carry = jnp.zeros((1, LANE), F32)
        for i in range(nblk):
            cs = _dot_rhs3(tri, lf_ref[0, i * BLK:(i + 1) * BLK, :]) + carry
            carry = cs[BLK - 1:BLK, :]
            pos = (i * BLK + row).astype(F32)
            acc = jnp.zeros((BLK, LANE), F32)
            for terms, mats in ((_split3(slope * pos), sel[0]), (_split3(cs * -LOG2E), sel[1])):
                for t, mat in zip(terms, mats):
                    acc = acc + _dot(t, mat)
            kbias[i * BLK:(i + 1) * BLK, :] = acc.astype(BF16)

    m_ref[...] = jnp.full(m_ref.shape, NEG, F32)
    l_ref[...] = jnp.zeros(l_ref.shape, F32)
    acc_ref[...] = jnp.zeros(acc_ref.shape, F32)

    qa = qa_ref[0].astype(F32)
    qb = qb_ref[0].astype(F32)
    for j in range(2):
        qa_t = qa[:, j * LANE:(j + 1) * LANE].T
        qb_t = qb[:, j * LANE:(j + 1) * LANE].T
        for half in range(2):
            h = 2 * j + half
            lo = half * HEAD_V
            w_alibi = jnp.where((row >= 3 * h) & (row < 3 * h + 3), 1.0, 0.0).astype(BF16)
            w_fox = jnp.where((row >= BIAS_F0 + 3 * h) & (row < BIAS_F0 + 3 * h + 3), 1.0, 0.0).astype(BF16)
            for m in range(2):
                keep = (row >= lo + m * DIFF_DQK) & (row < lo + (m + 1) * DIFF_DQK)
                c0 = (2 * half + m) * BLK
                wqa[j, 0:LANE, c0:c0 + BLK] = jnp.where(keep, qa_t, 0.0).astype(BF16)
                wqa[j, LANE:2 * LANE, c0:c0 + BLK] = w_alibi
            keep = (row >= lo) & (row < lo + HEAD_V)
            wqb[j, 0:LANE, half * BLK:(half + 1) * BLK] = jnp.where(keep, qb_t, 0.0).astype(BF16)
            wqb[j, LANE:2 * LANE, half * BLK:(half + 1) * BLK] = w_fox

    def scores(kj):
        ks = pl.multiple_of(kj * BLK, BLK)
        bias = kbias[pl.ds(ks, BLK), :]
        parts = []
        for k_ref, w_ref in ((ka_ref, wqa), (kb_ref, wqb)):
            for j in range(2):
                kk = jnp.concatenate([k_ref[0, pl.ds(ks, BLK), j * LANE:(j + 1) * LANE], bias], axis=1)
                parts.append(_dot(kk, w_ref[j]))
        return jnp.concatenate(parts, axis=1)

    def update(kj, s):
        ks = pl.multiple_of(kj * BLK, BLK)
        m_prev = m_ref[...]
        m_new = jnp.maximum(m_prev, jnp.max(s, axis=0, keepdims=True))
        alpha = jnp.exp2(m_prev - m_new)
        p = jnp.exp2(s - m_new)
        l_ref[...] = alpha * l_ref[...] + jnp.sum(p, axis=0, keepdims=True)
        m_ref[...] = m_new
        pb = p.astype(BF16)
        for h in range(N_HEADS):
            vta = vt_ref[h * HEAD_V:(h + 1) * HEAD_V, pl.ds(ks, BLK)]
            vtb = vt_ref[GROUP_W + h * HEAD_V:GROUP_W + (h + 1) * HEAD_V, pl.ds(ks, BLK)]
            for c0, w, vt in ((_map_lanes(h, 0), 2 * BLK, vta), (_map_lanes(h, 2), BLK, vtb)):
                acc_ref[:, c0:c0 + w] = alpha[:, c0:c0 + w] * acc_ref[:, c0:c0 + w] + _dot(vt, pb[:, c0:c0 + w])

    def pair(i, carry):
        s0 = scores(2 * i)
        s1 = scores(2 * i + 1)
        update(2 * i, s0)
        update(2 * i + 1, s1)
        return carry

    lax.fori_loop(0, qi // 2, pair, 0)

    @pl.when(qi % 2 == 1)
    def _():
        update(qi - 1, scores(qi - 1))

    key = _iota((BLK, S_W), 0)
    query = _iota((BLK, S_W), 1) % BLK
    update(qi, jnp.where(key <= query, scores(qi), NEG))

    lam = _lam(lp_ref, lam_init)
    mavg = _seg_mean_matrix(LANE, HEAD_V)
    g = g_ref[...]
    inv_l = 1.0 / l_ref[...]

    def head_out(h, m):
        c0 = _map_lanes(h, m)
        return acc_ref[:, c0:c0 + BLK] * inv_l[:, c0:c0 + BLK]

    for j in range(2):
        oa, ob = [], []
        for half in range(2):
            h = 2 * j + half
            oa.append(head_out(h, 0) - lam * head_out(h, 1))
            ob.append(head_out(h, 2))
        oa_j = jnp.concatenate(oa, axis=0).T
        ob_j = jnp.concatenate(ob, axis=0).T
        ga = g[:, j * LANE:(j + 1) * LANE] * (1.0 - lam_init)
        gb = g[:, GROUP_W + j * LANE:GROUP_W + (j + 1) * LANE]
        o_ref[0, :, j * LANE:(j + 1) * LANE] = _head_rms(oa_j, mavg, ga).astype(o_ref.dtype)
        o_ref[0, :, GROUP_W + j * LANE:GROUP_W + (j + 1) * LANE] = _head_rms(ob_j, mavg, gb).astype(o_ref.dtype)


def _attn_prompt(qkv, vt, lg, lam_params_l, g_out_l, lam_init):
    b, lp, _ = qkv.shape
    nblk = lp // BLK
    assert nblk <= 256
    qspec = lambda c: pl.BlockSpec((1, BLK, GROUP_W), lambda i, j: (i, j, c))
    kspec = lambda c: pl.BlockSpec((1, lp, GROUP_W), lambda i, j: (i, 0, c))
    return pl.pallas_call(
        functools.partial(_attn_kernel, lam_init=lam_init, nblk=nblk),
        grid=(b, nblk),
        in_specs=[qspec(0), kspec(1), qspec(3), kspec(4),
                  pl.BlockSpec((2 * GROUP_W, lp), lambda i, j: (0, i)),
                  pl.BlockSpec((1, lp, LANE), lambda i, j: (i, 0, 0)),
                  _const_spec((4, DIFF_DQK)), _const_spec((1, 1024))],
        out_specs=pl.BlockSpec((1, BLK, 2 * GROUP_W), lambda i, j: (i, j, 0)),
        out_shape=jax.ShapeDtypeStruct((b, lp, 2 * GROUP_W), BF16),
        scratch_shapes=[pltpu.VMEM((lp, LANE), BF16),
                        pltpu.VMEM((2, 2 * LANE, A_W), BF16), pltpu.VMEM((2, 2 * LANE, B_W), BF16),
                        pltpu.VMEM((1, S_W), F32), pltpu.VMEM((1, S_W), F32), pltpu.VMEM((HEAD_V, S_W), F32)],
        compiler_params=_params(("parallel", "arbitrary")),
        name="attn_prompt",
    )(qkv, qkv, qkv, qkv, vt, lg, lam_params_l, g_out_l)


def _lane_const(values, width, seg):
    lane = _iota((1, width), 1) // seg
    out = jnp.full((1, width), values[-1], F32)
    for h in range(len(values) - 2, -1, -1):
        out = jnp.where(lane == h, values[h], out)
    return out


def _state_rows(s_bd):
    hsum = s_bd[:, 0:LANE] + s_bd[:, LANE:2 * LANE]
    return (hsum + pltpu.roll(hsum, HEAD_V, 1))[:, 0:HEAD_V]


def _lin_kernel(cd_ref, g_ref, sc0_ref, sd0_ref, gout_ref, cat_ref, sc_ref, sd_ref, s_c, s_d, o_s,
                *, valid_len, n_tiles, nchunks):
    c = pl.program_id(1)
    bd = _block_diag_mask()
    e2 = bd.astype(BF16)

    @pl.when(c == 0)
    def _():
        expand = jnp.where(_iota((HEAD_V, GROUP_W), 0) == _iota((HEAD_V, GROUP_W), 1) % HEAD_V,
                           1.0, 0.0).astype(BF16)
        s_c[...] = _dot_lhs3(sc0_ref[0], expand) * bd
        s_d[...] = _dot_lhs3(sd0_ref[0], expand) * bd

    rowi = _iota((BLK, 1), 0)
    valid = (c * BLK + rowi) < valid_len
    n_v = jnp.clip(valid_len - c * BLK, 0, BLK).astype(F32)
    x = cd_ref[0]
    g = jnp.where(valid, g_ref[0], 0.0)
    q_c = x[:, 0:128] * LIN_SCALE
    k_c = jnp.where(valid, x[:, 128:256], 0.0)
    v_c = x[:, 256:512]
    r_c = x[:, 512:768]
    q_d = x[:, 768:896]
    k_d = jnp.where(valid, x[:, 896:1024], 0.0) * LIN_SCALE
    v_d = x[:, 1024:1280]
    r_d = x[:, 1280:1536]

    b = _dot_rhs3(_tri_incl(BLK), g)
    b_last = b[BLK - 1:BLK, :]
    s_prev = s_c[...]
    o_s[...] = _dot((q_c * jnp.exp(b)).astype(BF16), s_prev.astype(BF16))
    for ti in range(n_tiles):
        t0 = 8 * ti
        ns = t0 + 8
        ws = []
        for r in range(8):
            t = t0 + r
            diff = b[t:t + 1, :] - b[0:ns, :]
            dec = jnp.exp(jnp.where(rowi[0:ns] <= t, diff, NEG))
            ws.append(dec * (k_c[0:ns] * q_c[t:t + 1, :]))
        att = _dot(jnp.concatenate(ws, axis=0).astype(BF16), e2)
        o_t = jnp.sum(att.reshape(8, ns, GROUP_W) * v_c[0:ns][None], axis=1)
        o_s[t0:t0 + 8, :] = o_s[t0:t0 + 8, :] + o_t
    o_c = o_s[...]
    kt = k_c * jnp.exp(b_last - b)
    d_col = jnp.broadcast_to(jnp.exp(b_last), (LANE, LANE)).T
    s_new = s_prev * jnp.concatenate([d_col, d_col], axis=1) + _dot(kt.T.astype(BF16), v_c.astype(BF16)) * bd
    s_c[...] = s_new

    lgl = _lane_const(LOG_GAMMA, LANE, LIN_DK)
    rowf = rowi.astype(F32)
    d_prev = s_d[...]
    o_d = _dot((q_d * jnp.exp((rowf + 1.0) * lgl)).astype(BF16), d_prev.astype(BF16))
    lane_h = _iota((BLK, LANE), 1) // LIN_DK
    col_h = _iota((BLK, GROUP_W), 1) // HEAD_V
    rr = _iota((BLK, BLK), 0)
    cc = _iota((BLK, BLK), 1)
    dist = (rr - cc).astype(F32)
    kdb = k_d.astype(BF16)
    vdb = v_d.astype(BF16)
    for h in range(N_HEADS):
        qm = jnp.where(lane_h == h, q_d, 0.0).astype(BF16)
        dec = jnp.exp(jnp.where(rr >= cc, dist * LOG_GAMMA[h], NEG))
        att = (_dot_nt(qm, kdb) * dec).astype(BF16)
        o_d = o_d + jnp.where(col_h == h, _dot(att, vdb), 0.0)
    deck = jnp.where(valid, jnp.exp((n_v - 1.0 - rowf) * lgl), 0.0)
    lgr = jnp.full((LANE, 1), LOG_GAMMA[-1], F32)
    rrow = _iota((LANE, 1), 0) // LIN_DK
    for h in range(N_HEADS - 2, -1, -1):
        lgr = jnp.where(rrow == h, LOG_GAMMA[h], lgr)
    d_new = d_prev * jnp.exp(n_v * lgr) + _dot((k_d * deck).T.astype(BF16), vdb) * bd
    s_d[...] = d_new

    mavg = _seg_mean_matrix(GROUP_W, HEAD_V)
    gout = gout_ref[...]
    cat_ref[0, :, 0:GROUP_W] = (_silu(r_c) * _head_rms(o_c, mavg, gout[:, 512:768])).astype(cat_ref.dtype)
    cat_ref[0, :, GROUP_W:2 * GROUP_W] = (_silu(r_d) * _head_rms(o_d, mavg, gout[:, 768:1024])).astype(cat_ref.dtype)

    @pl.when(c == nchunks - 1)
    def _():
        sc_ref[0] = _state_rows(s_new)
        sd_ref[0] = _state_rows(d_new)


def _lin(cd, lg, sc0, sd0, g_out_l, valid_len, cat_dtype):
    b, lp, _ = cd.shape
    nchunks = lp // BLK
    n_tiles = BLK // 8 if nchunks > 1 else -(-valid_len // 8)
    st = pl.BlockSpec((1, LANE, HEAD_V), lambda i, j: (i, 0, 0))
    return pl.pallas_call(
        functools.partial(_lin_kernel, valid_len=valid_len, n_tiles=n_tiles, nchunks=nchunks),
        grid=(b, nchunks),
        in_specs=[pl.BlockSpec((1, BLK, W_CD), lambda i, j: (i, j, 0)),
                  pl.BlockSpec((1, BLK, LANE), lambda i, j: (i, j, 1)),
                  st, st, _const_spec((1, 1024))],
        out_specs=[pl.BlockSpec((1, BLK, 2 * GROUP_W), lambda i, j: (i, j, 0)), st, st],
        out_shape=[jax.ShapeDtypeStruct((b, lp, 2 * GROUP_W), cat_dtype),
                   jax.ShapeDtypeStruct((b, LANE, HEAD_V), F32),
                   jax.ShapeDtypeStruct((b, LANE, HEAD_V), F32)],
        scratch_shapes=[pltpu.VMEM((LANE, GROUP_W), F32), pltpu.VMEM((LANE, GROUP_W), F32),
                        pltpu.VMEM((BLK, GROUP_W), F32)],
        compiler_params=_params(("parallel", "arbitrary")),
        name="lin",
    )(cd, lg, sc0, sd0, g_out_l)


FF_CHUNK = 256


def _post_kernel(x_ref, cab_ref, ccd_ref, wo_ref, g1_ref, b1_ref, wfi_ref, wfo_ref, g2_ref, b2_ref, o_ref,
                 act_ref, *, alpha):
    y = (_dot(cab_ref[...].astype(BF16), wo_ref[0:512, :])
         + _dot(ccd_ref[...].astype(BF16), wo_ref[512:1024, :]))
    x1 = _layer_norm(alpha * x_ref[...] + y, g1_ref[...], b1_ref[...])
    x1b = x1.astype(BF16)
    for j in range(D_FF // FF_CHUNK):
        gate = _dot(x1b, wfi_ref[:, j * FF_CHUNK:(j + 1) * FF_CHUNK])
        up = _dot(x1b, wfi_ref[:, D_FF + j * FF_CHUNK:D_FF + (j + 1) * FF_CHUNK])
        act_ref[:, j * FF_CHUNK:(j + 1) * FF_CHUNK] = (_silu(gate) * up).astype(BF16)
    y2 = _dot(act_ref[...], wfo_ref[...])
    o_ref[...] = _layer_norm(alpha * x1 + y2, g2_ref[...], b2_ref[...])


def _post(x, cab, ccd, wo, g1, b1, wfi, wfo, g2, b2, tm, alpha):
    t = x.shape[0]
    row = lambda n: pl.BlockSpec((tm, n), lambda i: (i, 0))
    vec = _const_spec((1, D_MODEL))
    return pl.pallas_call(
        functools.partial(_post_kernel, alpha=alpha),
        grid=(t // tm,),
        in_specs=[row(D_MODEL), row(512), row(512), _const_spec((D_MODEL, D_MODEL)), vec, vec,
                  _const_spec((D_MODEL, 2 * D_FF)), _const_spec((D_FF, D_MODEL)), vec, vec],
        out_specs=row(D_MODEL),
        out_shape=jax.ShapeDtypeStruct((t, D_MODEL), F32),
        scratch_shapes=[pltpu.VMEM((tm, D_FF), BF16)],
        compiler_params=_params(("parallel",)),
        name="post",
    )(x, cab, ccd, wo, g1, b1, wfi, wfo, g2, b2)


PAGES_PER_STEP = 16
NQ = 8


def _rep_rows(x4):
    n = x4.shape[1]
    return jnp.concatenate([jnp.broadcast_to(x4[h:h + 1, :], (NQ, n)) for h in range(N_HEADS)], axis=0)


def _dec_kernel(pt_ref, *refs, lam_init, past_len, n_steps):
    g_pages = PAGES_PER_STEP
    ka_refs = refs[0:g_pages]
    va_refs = refs[g_pages:2 * g_pages]
    kb_refs = refs[2 * g_pages:3 * g_pages]
    vb_refs = refs[3 * g_pages:4 * g_pages]
    lf_refs = refs[4 * g_pages:5 * g_pages]
    (q_ref, kvn_ref, lfr_ref, lfc_ref, lp_ref, g_ref, o_ref,
     qa_s, qb_s, m_s, l_s, acc_s, carry_s, nq_s) = refs[5 * g_pages:]
    step = pl.program_id(1)
    n_pages = n_steps * g_pages
    width = g_pages * PAGE

    rows_a = 2 * N_HEADS * NQ
    rows_b = N_HEADS * NQ
    ra = _iota((rows_a, 1), 0)
    slope_a = jnp.full((rows_a, 1), SLOPES[-1], F32)
    for h in range(N_HEADS - 2, -1, -1):
        slope_a = jnp.where(ra // (2 * NQ) == h, SLOPES[h], slope_a)
    t_a = (past_len + ra % NQ).astype(F32)

    @pl.when(step == 0)
    def _():
        q = q_ref[...]
        qa = jnp.concatenate([q[:, 0:GROUP_W]] * (2 * N_HEADS), axis=0)
        r = _iota((rows_a, GROUP_W), 0)
        cidx = _iota((rows_a, GROUP_W), 1)
        keep = (cidx // HEAD_V == r // (2 * NQ)) & ((cidx % HEAD_V) // DIFF_DQK == (r // NQ) % 2)
        qa_s[...] = jnp.where(keep, qa, 0.0).astype(BF16)
        qb = jnp.concatenate([q[:, 768:1024]] * N_HEADS, axis=0)
        r = _iota((rows_b, GROUP_W), 0)
        cidx = _iota((rows_b, GROUP_W), 1)
        qb_s[...] = jnp.where(cidx // HEAD_V == r // NQ, qb, 0.0).astype(BF16)
        m_s[...] = jnp.full(m_s.shape, NEG, F32)
        l_s[...] = jnp.zeros(l_s.shape, F32)
        acc_s[...] = jnp.zeros(acc_s.shape, F32)
        carry_s[...] = jnp.zeros(carry_s.shape, F32)
        cum = _dot_rhs3(_tri_incl(NQ), lfc_ref[0])
        nq_s[...] = jnp.concatenate(
            [jnp.broadcast_to(cum[:, h:h + 1], (NQ, LANE)) for h in range(N_HEADS)], axis=0)

    def update(s, va_t, vb_t, nt):
        m_prev = m_s[...]
        m_new = jnp.maximum(m_prev, jnp.max(s, axis=1, keepdims=True))
        alpha = jnp.exp(m_prev - m_new)
        p = jnp.exp(s - m_new)
        l_s[...] = alpha * l_s[...] + jnp.sum(p, axis=1, keepdims=True)
        pb = p.astype(BF16)
        if nt:
            pv_a = _dot_nt(pb[0:rows_a], va_t)
            pv_b = _dot_nt(pb[rows_a:], vb_t)
        else:
            pv_a = _dot(pb[0:rows_a], va_t)
            pv_b = _dot(pb[rows_a:], vb_t)
        acc_s[...] = alpha * acc_s[...] + jnp.concatenate([pv_a, pv_b], axis=0)
        m_s[...] = m_new

    pages = lambda rs: jnp.concatenate([r[0, 0].reshape(GROUP_W, PAGE).astype(BF16) for r in rs], axis=1)
    lane = _iota((1, width), 1)
    pos = ((n_pages - 1 - step * g_pages - lane // PAGE) * PAGE + lane % PAGE).astype(F32)
    strict = jnp.where(_iota((LANE, LANE), 0) > _iota((LANE, LANE), 1), 1.0, 0.0).astype(BF16)
    carry = carry_s[...]
    parts = []
    for r in range(g_pages):
        lf = jnp.concatenate([lf_refs[r][0, 0], jnp.zeros((8 - N_HEADS, PAGE), F32)], axis=0)
        parts.append(_dot_lhs3(lf, strict) + carry)
        carry = carry + jnp.sum(lf, axis=1, keepdims=True)
    carry_s[...] = carry
    csuf = jnp.concatenate(parts, axis=1)
    nq = jnp.concatenate([nq_s[...]] * g_pages, axis=1)
    s_a = _dot(qa_s[...], pages(ka_refs)) - slope_a * (t_a - pos)
    s_b = _dot(qb_s[...], pages(kb_refs)) + nq + _rep_rows(csuf)
    update(jnp.concatenate([s_a, s_b], axis=0), pages(va_refs), pages(vb_refs), True)

    @pl.when(step == n_steps - 1)
    def _():
        kvn = kvn_ref[...]
        zpad = jnp.zeros((PAGE - NQ, GROUP_W), F32)
        kn_a = jnp.concatenate([kvn[:, 0:256], zpad], axis=0).astype(BF16)
        vn_a = jnp.concatenate([kvn[:, 256:512], zpad], axis=0).astype(BF16)
        kn_b = jnp.concatenate([kvn[:, 512:768], zpad], axis=0).astype(BF16)
        vn_b = jnp.concatenate([kvn[:, 768:1024], zpad], axis=0).astype(BF16)
        jl = _iota((1, LANE), 1)
        ok_a = jl <= ra % NQ
        rb = _iota((rows_b, 1), 0)
        ok_b = jl <= rb % NQ
        cum_row = _dot_lhs3(lfr_ref[0], jnp.where(_iota((LANE, LANE), 0) <= _iota((LANE, LANE), 1),
                                                  1.0, 0.0).astype(BF16))
        sn_a = _dot_nt(qa_s[...], kn_a) - slope_a * ((ra % NQ).astype(F32) - jl.astype(F32))
        sn_b = _dot_nt(qb_s[...], kn_b) + nq_s[...] - _rep_rows(cum_row)
        update(jnp.concatenate([jnp.where(ok_a, sn_a, NEG), jnp.where(ok_b, sn_b, NEG)], axis=0),
               vn_a, vn_b, False)

        lam = _lam(lp_ref, lam_init)
        acc = acc_s[...] / l_s[...]
        col_h = _iota((NQ, GROUP_W), 1) // HEAD_V
        o_a = jnp.zeros((NQ, GROUP_W), F32)
        o_b = jnp.zeros((NQ, GROUP_W), F32)
        for h in range(N_HEADS):
            r0 = 2 * NQ * h
            o_a = o_a + jnp.where(col_h == h, acc[r0:r0 + NQ] - lam * acc[r0 + NQ:r0 + 2 * NQ], 0.0)
            rb0 = rows_a + NQ * h
            o_b = o_b + jnp.where(col_h == h, acc[rb0:rb0 + NQ], 0.0)
        mavg = _seg_mean_matrix(GROUP_W, HEAD_V)
        g = g_ref[...]
        o_ref[:, 0:GROUP_W] = _head_rms(o_a, mavg, g[:, 0:256] * (1.0 - lam_init))
        o_ref[:, GROUP_W:2 * GROUP_W] = _head_rms(o_b, mavg, g[:, 256:512])


def _attn_sample(page_table, caches, layer, qkv, kvn, lf_row, lf_col, lam_params_l, g_out_l, lam_init):
    ck_a, cv_a, ck_b, cv_b, c_lf = caches
    nb, n_pages = page_table.shape
    g_pages = PAGES_PER_STEP
    assert n_pages % g_pages == 0
    n_steps = n_pages // g_pages
    past_len = n_pages * PAGE

    def page_spec(r, lf=False):
        def imap(i, s, pt):
            return (layer, pt[i, n_pages - 1 - (s * g_pages + r)], 0, 0) + (() if lf else (0,))
        return pl.BlockSpec((1, 1, N_HEADS, PAGE) if lf else (1, 1, N_HEADS, HEAD_V, PAGE), imap)

    tok = lambda n: pl.BlockSpec((NQ, n), lambda i, s, pt: (i, 0))
    small = lambda: pl.BlockSpec((1, 8, LANE), lambda i, s, pt: (i, 0, 0))
    in_specs = ([page_spec(r) for r in range(g_pages)] * 4 + [page_spec(r, True) for r in range(g_pages)]
                + [tok(W_AB), tok(1024), small(), small(),
                   pl.BlockSpec((4, DIFF_DQK), lambda i, s, pt: (0, 0)),
                   pl.BlockSpec((1, 1024), lambda i, s, pt: (0, 0))])
    rows = 3 * N_HEADS * NQ
    return pl.pallas_call(
        functools.partial(_dec_kernel, lam_init=lam_init, past_len=past_len, n_steps=n_steps),
        grid_spec=pltpu.PrefetchScalarGridSpec(
            num_scalar_prefetch=1,
            grid=(nb, n_steps),
            in_specs=in_specs,
            out_specs=pl.BlockSpec((NQ, 2 * GROUP_W), lambda i, s, pt: (i, 0)),
            scratch_shapes=[pltpu.VMEM((2 * N_HEADS * NQ, GROUP_W), BF16),
                            pltpu.VMEM((N_HEADS * NQ, GROUP_W), BF16),
                            pltpu.VMEM((rows, 1), F32), pltpu.VMEM((rows, 1), F32),
                            pltpu.VMEM((rows, GROUP_W), F32),
                            pltpu.VMEM((8, 1), F32), pltpu.VMEM((N_HEADS * NQ, LANE), F32)]),
        out_shape=jax.ShapeDtypeStruct((nb * NQ, 2 * GROUP_W), F32),
        compiler_params=_params(("parallel", "arbitrary")),
        name="attn_sample",
    )(page_table, *([ck_a] * g_pages + [cv_a] * g_pages + [ck_b] * g_pages + [cv_b] * g_pages
                    + [c_lf] * g_pages), qkv, kvn, lf_row, lf_col, lam_params_l, g_out_l)


def kernel(x_prompt, x_sample, cache_a_k, cache_a_v, cache_b_k, cache_b_v, cache_b_logf, state_c, state_d,
           page_table, meta_tokens, w_in, b_f, w_c_gate, b_c_gate, lam_params, g_out, w_out, ln1_g, ln1_b,
           w_ffn_in, w_ffn_out, ln2_g, ln2_b):
    depth = w_in.shape[0]
    alpha = (2.0 * depth) ** 0.25
    nb, seq, _ = x_prompt.shape
    db, nq, _ = x_sample.shape
    assert nq == NQ
    l_true = N_META + seq
    lp = -(-l_true // BLK) * BLK
    xp = jnp.concatenate([jnp.broadcast_to(meta_tokens.astype(F32), (nb, N_META, D_MODEL)), x_prompt,
                          jnp.zeros((nb, lp - l_true, D_MODEL), F32)], axis=1).reshape(nb * lp, D_MODEL)
    xs = x_sample.reshape(db * nq, D_MODEL)
    tm_p = 512 if (nb * lp) % 512 == 0 else BLK
    tm_s = db * nq

    tr = lambda c: jnp.transpose(c, (0, 1, 3, 4, 2))
    caches = (tr(cache_a_k), tr(cache_a_v), tr(cache_b_k), tr(cache_b_v), jnp.transpose(cache_b_logf, (0, 1, 3, 2)))

    kvs, lgs, p_states, s_rows = [], [], [], []
    for l in range(depth):
        lam_init = 0.8 - 0.6 * math.exp(-0.3 * l)
        w = _prep_w_in(w_in[l])
        bf = jnp.pad(b_f[l], (0, LANE - N_HEADS)).reshape(1, LANE)
        wcg = jnp.pad(w_c_gate[l], ((0, LANE - GLA_RANK), (0, 0)))
        bcg = b_c_gate[l].reshape(1, LANE)
        gl = g_out[l].reshape(1, 1024)
        wo = w_out[l].astype(BF16)
        wfi = w_ffn_in[l].astype(BF16)
        wfo = w_ffn_out[l].astype(BF16)
        vec = lambda a: a[l].reshape(1, D_MODEL)
        post = functools.partial(_post, wo=wo, g1=vec(ln1_g), b1=vec(ln1_b), wfi=wfi, wfo=wfo,
                                 g2=vec(ln2_g), b2=vec(ln2_b), alpha=alpha)

        qkv, kv, cd, lg, vt = _proj(xp, w, bf, wcg, bcg, tm_p, BF16, LOG2E)
        lg3 = lg.reshape(nb, lp, 2 * LANE)
        kvs.append(kv.reshape(nb, lp, 1024))
        lgs.append(lg3)
        cab = _attn_prompt(qkv.reshape(nb, lp, W_AB), vt, lg3, lam_params[l], gl, lam_init)
        zero_state = jnp.zeros((nb, LANE, HEAD_V), F32)
        ccd, sc_p, sd_p = _lin(cd.reshape(nb, lp, W_CD), lg3, zero_state, zero_state, gl, l_true, BF16)
        xp = post(xp, cab.reshape(nb * lp, 512), ccd.reshape(nb * lp, 512), tm=tm_p)
        p_states.append((sc_p.reshape(nb, N_HEADS, LIN_DK, HEAD_V), sd_p.reshape(nb, N_HEADS, LIN_DK, HEAD_V)))

        qkv_s, kv_s, cd_s, lg_s, _ = _proj(xs, w, bf, wcg, bcg, tm_s, F32, 1.0)
        lf_s = lg_s[:, 0:N_HEADS].reshape(db, nq, N_HEADS)
        lf_col = jnp.pad(lf_s, ((0, 0), (0, 0), (0, LANE - N_HEADS)))
        lf_row = jnp.pad(jnp.transpose(lf_s, (0, 2, 1)), ((0, 0), (0, 8 - N_HEADS), (0, LANE - nq)))
        cab_s = _attn_sample(page_table, caches, l, qkv_s, kv_s, lf_row, lf_col, lam_params[l], gl, lam_init)
        pad_rows = lambda a: jnp.pad(a.reshape(db, nq, a.shape[-1]), ((0, 0), (0, BLK - nq), (0, 0)))
        ccd_s, sc_s, sd_s = _lin(pad_rows(cd_s), pad_rows(lg_s), state_c[l].reshape(db, LANE, HEAD_V),
                                 state_d[l].reshape(db, LANE, HEAD_V), gl, nq, F32)
        xs = post(xs, cab_s, ccd_s[:, 0:nq].reshape(db * nq, 512), tm=tm_s)
        heads_s = lambda a: a.reshape(db, nq, N_HEADS, HEAD_V)
        s_rows.append((heads_s(kv_s[:, 0:256]), heads_s(kv_s[:, 256:512]), heads_s(kv_s[:, 512:768]),
                       heads_s(kv_s[:, 768:1024]), lf_s,
                       sc_s.reshape(db, N_HEADS, LIN_DK, HEAD_V), sd_s.reshape(db, N_HEADS, LIN_DK, HEAD_V)))

    y_prompt = xp.reshape(nb, lp, D_MODEL)[:, N_META:l_true]
    y_sample = xs.reshape(db, nq, D_MODEL)
    ka_t, va_t, kb_t, vb_t, lf_t = _rows(kvs, lgs, l_true)
    heads_t = lambda a: jnp.transpose(a.reshape(depth, nb, N_HEADS, HEAD_V, l_true), (0, 1, 4, 2, 3))
    p_out = [heads_t(ka_t), heads_t(va_t), heads_t(kb_t), heads_t(vb_t), jnp.transpose(lf_t, (0, 1, 3, 2))]
    p_out += [jnp.stack(z) for z in zip(*p_states)]
    s_out = [jnp.stack(z) for z in zip(*s_rows)]
    return (y_prompt, y_sample, *p_out, *s_out)
```

```python
import functools
import math

import jax
import jax.numpy as jnp
from jax import lax
from jax.experimental import pallas as pl
from jax.experimental.pallas import tpu as pltpu

F32 = jnp.float32
BF16 = jnp.bfloat16

D_MODEL = 1024
N_META = 16
N_HEADS = 4
HEAD_V = 64
GROUP_W = 256
DIFF_DQK = 32
LIN_DK = 32
GLA_RANK = 16
GLA_TAU = 16.0
D_FF = 2816
EPS = 1e-5
NEG = -1e30
PAGE = 128
BLK = 128
LANE = 128
A_SCALE = DIFF_DQK ** -0.5
B_SCALE = HEAD_V ** -0.5
LIN_SCALE = LIN_DK ** -0.5
SLOPES = tuple(2.0 ** (-8.0 * (h + 1.0) / N_HEADS) for h in range(N_HEADS))
LOG_GAMMA = tuple(math.log1p(-(2.0 ** (-5.0 - h))) for h in range(N_HEADS))
LOG2E = math.log2(math.e)
IN_SIZES = (256, 256, 256, 256, 256, 256, 4, 128, 128, 256, 256, 16, 128, 128, 256, 256)
VMEM_LIMIT = 56 * 1024 * 1024


def _dot(a, b):
    return jnp.dot(a, b, preferred_element_type=F32)


def _dot_nt(a, b):
    return lax.dot_general(a, b, (((1,), (1,)), ((), ())), preferred_element_type=F32)


def _split3(x):
    hi = x.astype(BF16)
    r = x - hi.astype(F32)
    mid = r.astype(BF16)
    lo = (r - mid.astype(F32)).astype(BF16)
    return hi, mid, lo


def _dot_lhs3(x, m):
    hi, mid, lo = _split3(x)
    return _dot(hi, m) + _dot(mid, m) + _dot(lo, m)


def _dot_rhs3(m, x):
    hi, mid, lo = _split3(x)
    return _dot(m, hi) + _dot(m, mid) + _dot(m, lo)


def _iota(shape, dim):
    return lax.broadcasted_iota(jnp.int32, shape, dim)


def _tri_incl(n):
    return jnp.where(_iota((n, n), 0) >= _iota((n, n), 1), 1.0, 0.0).astype(BF16)


def _seg_mean_matrix(width, seg):
    same = (_iota((width, width), 0) // seg) == (_iota((width, width), 1) // seg)
    return jnp.where(same, 1.0 / seg, 0.0).astype(BF16)


def _block_diag_mask():
    r = _iota((LANE, GROUP_W), 0) // LIN_DK
    c = _iota((LANE, GROUP_W), 1) // HEAD_V
    return jnp.where(r == c, 1.0, 0.0).astype(F32)


def _log_sigmoid(x):
    return jnp.minimum(x, 0.0) - jnp.log1p(jnp.exp(-jnp.abs(x)))


def _silu(x):
    return x / (1.0 + jnp.exp(-x))


def _head_rms(o, mavg, gain):
    ms = _dot_lhs3(o * o, mavg)
    return o * lax.rsqrt(ms + EPS) * gain


def _layer_norm(x, g, b):
    mu = jnp.mean(x, axis=-1, keepdims=True)
    xc = x - mu
    var = jnp.mean(xc * xc, axis=-1, keepdims=True)
    return xc * lax.rsqrt(var + EPS) * g + b


def _lam(lp_ref, lam_init):
    lp = lp_ref[...]
    a = jnp.sum(lp[0:1, :] * lp[1:2, :], axis=1, keepdims=True)
    b = jnp.sum(lp[2:3, :] * lp[3:4, :], axis=1, keepdims=True)
    return jnp.exp(a) - jnp.exp(b) + lam_init


def _params(sem):
    return pltpu.CompilerParams(dimension_semantics=sem, vmem_limit_bytes=VMEM_LIMIT)


def _const_spec(shape):
    n = len(shape)
    return pl.BlockSpec(shape, lambda *_: (0,) * n, pipeline_mode=pl.Buffered(1))


W_AB = 1536
W_CD = 1536
W_FG = 256
W_ALL = W_AB + W_CD + W_FG


def _prep_w_in(w_in_t):
    offs = [0]
    for s in IN_SIZES:
        offs.append(offs[-1] + s)
    seg = lambda i: w_in_t[:, offs[i]:offs[i + 1], :]
    pad = lambda a: jnp.pad(a, ((0, 0), (0, LANE - a.shape[1]), (0, 0)))
    rows = [seg(i) for i in (0, 1, 2, 3, 4, 5)] + [seg(i) for i in (7, 8, 9, 10, 12, 13, 14, 15)]
    rows += [pad(seg(6)), pad(seg(11))]
    return jnp.concatenate(rows, axis=1).astype(BF16)


def _proj_kernel(x_ref, w_ref, bf_ref, wcg_ref, bcg_ref, qkv_ref, kv_ref, cd_ref, lg_ref, vt_ref, *, q_scale):
    xb = x_ref[...].astype(BF16)
    ab = _dot_nt(xb, w_ref[0:W_AB, :])
    dt = qkv_ref.dtype
    qkv_ref[:, 0:256] = (ab[:, 0:256] * (A_SCALE * q_scale)).astype(dt)
    qkv_ref[:, 256:768] = ab[:, 256:768].astype(dt)
    qkv_ref[:, 768:1024] = (ab[:, 768:1024] * (B_SCALE * q_scale)).astype(dt)
    qkv_ref[:, 1024:1536] = ab[:, 1024:1536].astype(dt)
    kv_ref[:, 0:512] = ab[:, 256:768]
    kv_ref[:, 512:1024] = ab[:, 1024:1536]
    vt_ref[0:GROUP_W, :] = ab[:, 512:768].T.astype(BF16)
    vt_ref[GROUP_W:2 * GROUP_W, :] = ab[:, 1280:1536].T.astype(BF16)
    cd_ref[...] = _dot_nt(xb, w_ref[W_AB:W_AB + W_CD, :])
    fa = _dot_nt(xb, w_ref[W_AB + W_CD:W_ALL, :])
    lg_ref[:, 0:LANE] = _log_sigmoid(fa[:, 0:LANE] + bf_ref[...])
    a = fa[:, LANE:2 * LANE]
    a_hi = a.astype(BF16)
    a_lo = (a - a_hi.astype(F32)).astype(BF16)
    w = wcg_ref[...]
    w_hi = w.astype(BF16)
    w_lo = (w - w_hi.astype(F32)).astype(BF16)
    z = _dot(a_hi, w_hi) + _dot(a_hi, w_lo) + _dot(a_lo, w_hi) + bcg_ref[...]
    lg_ref[:, LANE:2 * LANE] = _log_sigmoid(z) / GLA_TAU


def _proj(x, w, bf, wcg, bcg, tm, qkv_dtype, q_scale):
    t = x.shape[0]
    row = lambda n: pl.BlockSpec((tm, n), lambda i: (i, 0))
    return pl.pallas_call(
        functools.partial(_proj_kernel, q_scale=q_scale),
        grid=(t // tm,),
        in_specs=[row(D_MODEL), _const_spec((W_ALL, D_MODEL)), _const_spec((1, LANE)),
                  _const_spec((LANE, LANE)), _const_spec((1, LANE))],
        out_specs=[row(W_AB), row(1024), row(W_CD), row(2 * LANE),
                   pl.BlockSpec((2 * GROUP_W, tm), lambda i: (0, i))],
        out_shape=[jax.ShapeDtypeStruct((t, W_AB), qkv_dtype),
                   jax.ShapeDtypeStruct((t, 1024), F32),
                   jax.ShapeDtypeStruct((t, W_CD), F32),
                   jax.ShapeDtypeStruct((t, 2 * LANE), F32),
                   jax.ShapeDtypeStruct((2 * GROUP_W, t), BF16)],
        compiler_params=_params(("parallel",)),
        name="proj",
    )(x, w, bf, wcg, bcg)


ROWS_BLK = 512


def _rows_kernel(*refs, depth):
    kv_refs = refs[0:depth]
    lg_refs = refs[depth:2 * depth]
    ka_ref, va_ref, kb_ref, vb_ref, lf_ref = refs[2 * depth:]
    for l in range(depth):
        kv = kv_refs[l][0]
        ka_ref[l, 0] = kv[:, 0:256].T
        va_ref[l, 0] = kv[:, 256:512].T
        kb_ref[l, 0] = kv[:, 512:768].T
        vb_ref[l, 0] = kv[:, 768:1024].T
        lf_ref[l, 0] = lg_refs[l][0].T[0:N_HEADS, :]


def _rows(kvs, lgs, l_true):
    depth = len(kvs)
    b, lp, _ = kvs[0].shape
    rb = min(ROWS_BLK, lp)
    out = pl.BlockSpec((depth, 1, GROUP_W, rb), lambda i, j: (0, i, 0, j))
    return pl.pallas_call(
        functools.partial(_rows_kernel, depth=depth),
        grid=(b, pl.cdiv(l_true, rb)),
        in_specs=[pl.BlockSpec((1, rb, 1024), lambda i, j: (i, j, 0))] * depth
        + [pl.BlockSpec((1, rb, LANE), lambda i, j: (i, j, 0))] * depth,
        out_specs=[out, out, out, out, pl.BlockSpec((depth, 1, N_HEADS, rb), lambda i, j: (0, i, 0, j))],
        out_shape=[jax.ShapeDtypeStruct((depth, b, GROUP_W, l_true), F32)] * 4
        + [jax.ShapeDtypeStruct((depth, b, N_HEADS, l_true), F32)],
        compiler_params=_params(("parallel", "parallel")),
        name="rows",
    )(*kvs, *lgs)


BIAS_F0 = 3 * N_HEADS


A_W = 4 * BLK
B_W = 2 * BLK
S_W = 2 * A_W + 2 * B_W


def _map_lanes(h, m):
    j, half = divmod(h, 2)
    if m < 2:
        return j * A_W + (2 * half + m) * BLK
    return 2 * A_W + j * B_W + half * BLK


def _attn_kernel(qa_ref, ka_ref, qb_ref, kb_ref, vt_ref, lf_ref, lp_ref, g_ref, o_ref,
                 kbias, wqa, wqb, m_ref, l_ref, acc_ref, s_even, s_odd, *, lam_init, nblk):
    qi = pl.program_id(1)
    row = _iota((BLK, BLK), 0)
    col = _iota((BLK, BLK), 1)

    @pl.when(qi == 0)
    def _():
        tri = _tri_incl(BLK)
        sel = [[jnp.where((row < N_HEADS) & (col == base + 3 * row + i), 1.0, 0.0).astype(BF16)
                for i in range(3)] for base in (0, BIAS_F0)]
        slope = _lane_const(tuple(LOG2E * s for s in SLOPES) + (0.0,), LANE, 1)
        carry = jnp.zeros((1, LANE), F32)
        for i in range(nblk):
            cs = _dot_rhs3(tri, lf_ref[0, i * BLK:(i + 1) * BLK, :]) + carry
            carry = cs[BLK - 1:BLK, :]
            pos = (i * BLK + row).astype(F32)
            acc = jnp.zeros((BLK, LANE), F32)
            for terms, mats in ((_split3(slope * pos), sel[0]), (_split3(cs * -LOG2E), sel[1])):
                for t, mat in zip(terms, mats):
                    acc = acc + _dot(t, mat)
            kbias[i * BLK:(i + 1) * BLK, :] = acc.astype(BF16)

    m_ref[...] = jnp.full(m_ref.shape, NEG, F32)
    l_ref[...] = jnp.zeros(l_ref.shape, F32)
    acc_ref[...] = jnp.zeros(acc_ref.shape, F32)

    qa = qa_ref[0].astype(F32)
    qb = qb_ref[0].astype(F32)
    for j in range(2):
        qa_t = qa[:, j * LANE:(j + 1) * LANE].T
        qb_t = qb[:, j * LANE:(j + 1) * LANE].T
        for half in range(2):
            h = 2 * j + half
            lo = half * HEAD_V
            w_alibi = jnp.where((row >= 3 * h) & (row < 3 * h + 3), 1.0, 0.0).astype(BF16)
            w_fox = jnp.where((row >= BIAS_F0 + 3 * h) & (row < BIAS_F0 + 3 * h + 3), 1.0, 0.0).astype(BF16)
            for m in range(2):
                keep = (row >= lo + m * DIFF_DQK) & (row < lo + (m + 1) * DIFF_DQK)
                c0 = (2 * half + m) * BLK
                wqa[j, 0:LANE, c0:c0 + BLK] = jnp.where(keep, qa_t, 0.0).astype(BF16)
                wqa[j, LANE:2 * LANE, c0:c0 + BLK] = w_alibi
            keep = (row >= lo) & (row < lo + HEAD_V)
            wqb[j, 0:LANE, half * BLK:(half + 1) * BLK] = jnp.where(keep, qb_t, 0.0).astype(BF16)
            wqb[j, LANE:2 * LANE, half * BLK:(half + 1) * BLK] = w_fox

    def scores(kj):
        ks = pl.multiple_of(kj * BLK, BLK)
        bias = kbias[pl.ds(ks, BLK), :]
        parts = []
        for k_ref, w_ref in ((ka_ref, wqa), (kb_ref, wqb)):
            for j in range(2):
                kk = jnp.concatenate([k_ref[0, pl.ds(ks, BLK), j * LANE:(j + 1) * LANE], bias], axis=1)
                parts.append(_dot(kk, w_ref[j]))
        return jnp.concatenate(parts, axis=1)

    def update(kj, s):
        ks = pl.multiple_of(kj * BLK, BLK)
        m_prev = m_ref[...]
        m_new = jnp.maximum(m_prev, jnp.max(s, axis=0, keepdims=True))
        alpha = jnp.exp2(m_prev - m_new)
        p = jnp.exp2(s - m_new)
        l_ref[...] = alpha * l_ref[...] + jnp.sum(p, axis=0, keepdims=True)
        m_ref[...] = m_new
        pb = p.astype(BF16)
        for h in range(N_HEADS):
            vta = vt_ref[h * HEAD_V:(h + 1) * HEAD_V, pl.ds(ks, BLK)]
            vtb = vt_ref[GROUP_W + h * HEAD_V:GROUP_W + (h + 1) * HEAD_V, pl.ds(ks, BLK)]
            for c0, w, vt in ((_map_lanes(h, 0), 2 * BLK, vta), (_map_lanes(h, 2), BLK, vtb)):
                acc_ref[:, c0:c0 + w] = alpha[:, c0:c0 + w] * acc_ref[:, c0:c0 + w] + _dot(vt, pb[:, c0:c0 + w])

    def diag(s):
        return jnp.where(_iota((BLK, S_W), 0) <= _iota((BLK, S_W), 1) % BLK, s, NEG)

    s_even[...] = scores(0)

    def pair(i, carry):
        s_odd[...] = scores(2 * i + 1)
        update(2 * i, s_even[...])
        s_even[...] = scores(2 * i + 2)
        update(2 * i + 1, s_odd[...])
        return carry

    lax.fori_loop(0, qi // 2, pair, 0)

    @pl.when(qi % 2 == 0)
    def _():
        update(qi, diag(s_even[...]))

    @pl.when(qi % 2 == 1)
    def _():
        s_odd[...] = scores(qi)
        update(qi - 1, s_even[...])
        update(qi, diag(s_odd[...]))

    lam = _lam(lp_ref, lam_init)
    mavg = _seg_mean_matrix(LANE, HEAD_V)
    g = g_ref[...]
    inv_l = 1.0 / l_ref[...]

    def head_out(h, m):
        c0 = _map_lanes(h, m)
        return acc_ref[:, c0:c0 + BLK] * inv_l[:, c0:c0 + BLK]

    for j in range(2):
        oa, ob = [], []
        for half in range(2):
            h = 2 * j + half
            oa.append(head_out(h, 0) - lam * head_out(h, 1))
            ob.append(head_out(h, 2))
        oa_j = jnp.concatenate(oa, axis=0).T
        ob_j = jnp.concatenate(ob, axis=0).T
        ga = g[:, j * LANE:(j + 1) * LANE] * (1.0 - lam_init)
        gb = g[:, GROUP_W + j * LANE:GROUP_W + (j + 1) * LANE]
        o_ref[0, :, j * LANE:(j + 1) * LANE] = _head_rms(oa_j, mavg, ga).astype(o_ref.dtype)
        o_ref[0, :, GROUP_W + j * LANE:GROUP_W + (j + 1) * LANE] = _head_rms(ob_j, mavg, gb).astype(o_ref.dtype)


def _attn_prompt(qkv, vt, lg, lam_params_l, g_out_l, lam_init):
    b, lp, _ = qkv.shape
    nblk = lp // BLK
    assert nblk <= 256
    qspec = lambda c: pl.BlockSpec((1, BLK, GROUP_W), lambda i, j: (i, j, c))
    kspec = lambda c: pl.BlockSpec((1, lp, GROUP_W), lambda i, j: (i, 0, c))
    return pl.pallas_call(
        functools.partial(_attn_kernel, lam_init=lam_init, nblk=nblk),
        grid=(b, nblk),
        in_specs=[qspec(0), kspec(1), qspec(3), kspec(4),
                  pl.BlockSpec((2 * GROUP_W, lp), lambda i, j: (0, i)),
                  pl.BlockSpec((1, lp, LANE), lambda i, j: (i, 0, 0)),
                  _const_spec((4, DIFF_DQK)), _const_spec((1, 1024))],
        out_specs=pl.BlockSpec((1, BLK, 2 * GROUP_W), lambda i, j: (i, j, 0)),
        out_shape=jax.ShapeDtypeStruct((b, lp, 2 * GROUP_W), BF16),
        scratch_shapes=[pltpu.VMEM((lp, LANE), BF16),
                        pltpu.VMEM((2, 2 * LANE, A_W), BF16), pltpu.VMEM((2, 2 * LANE, B_W), BF16),
                        pltpu.VMEM((1, S_W), F32), pltpu.VMEM((1, S_W), F32), pltpu.VMEM((HEAD_V, S_W), F32),
                        pltpu.VMEM((BLK, S_W), F32), pltpu.VMEM((BLK, S_W), F32)],
        compiler_params=_params(("parallel", "arbitrary")),
        name="attn_prompt",
    )(qkv, qkv, qkv, qkv, vt, lg, lam_params_l, g_out_l)


SUB = 32


def _lane_const(values, width, seg):
    lane = _iota((1, width), 1) // seg
    out = jnp.full((1, width), values[-1], F32)
    for h in range(len(values) - 2, -1, -1):
        out = jnp.where(lane == h, values[h], out)
    return out


def _state_rows(s_bd):
    hsum = s_bd[:, 0:LANE] + s_bd[:, LANE:2 * LANE]
    return (hsum + pltpu.roll(hsum, HEAD_V, 1))[:, 0:HEAD_V]


def _lin_kernel(cd_ref, g_ref, sc0_ref, sd0_ref, gout_ref, cat_ref, sc_ref, sd_ref, s_c, s_d, o_s,
                *, valid_len, n_tiles, nchunks):
    c = pl.program_id(1)
    bd = _block_diag_mask()
    e2 = bd.astype(BF16)

    @pl.when(c == 0)
    def _():
        expand = jnp.where(_iota((HEAD_V, GROUP_W), 0) == _iota((HEAD_V, GROUP_W), 1) % HEAD_V,
                           1.0, 0.0).astype(BF16)
        s_c[...] = _dot_lhs3(sc0_ref[0], expand) * bd
        s_d[...] = _dot_lhs3(sd0_ref[0], expand) * bd

    rowi = _iota((BLK, 1), 0)
    valid = (c * BLK + rowi) < valid_len
    n_v = jnp.clip(valid_len - c * BLK, 0, BLK).astype(F32)
    x = cd_ref[0]
    g = jnp.where(valid, g_ref[0], 0.0)
    q_c = x[:, 0:128] * LIN_SCALE
    k_c = jnp.where(valid, x[:, 128:256], 0.0)
    v_c = x[:, 256:512]
    r_c = x[:, 512:768]
    q_d = x[:, 768:896]
    k_d = jnp.where(valid, x[:, 896:1024], 0.0) * LIN_SCALE
    v_d = x[:, 1024:1280]
    r_d = x[:, 1280:1536]

    b = _dot_rhs3(_tri_incl(BLK), g)
    b_last = b[BLK - 1:BLK, :]
    s_prev = s_c[...]
    o_s[...] = _dot((q_c * jnp.exp(b)).astype(BF16), s_prev.astype(BF16))
    lane_h = _iota((SUB, LANE), 1) // LIN_DK
    col_h = _iota((SUB, GROUP_W), 1) // HEAD_V
    vcb = v_c.astype(BF16)
    for sb in range(-(-n_tiles * 8 // SUB)):
        r0 = sb * SUB
        for ti in range(min(SUB // 8, n_tiles - r0 // 8)):
            t0 = r0 + 8 * ti
            ns = 8 * (ti + 1)
            ws = []
            for r in range(8):
                t = t0 + r
                diff = b[t:t + 1, :] - b[r0:r0 + ns, :]
                dec = jnp.exp(jnp.where(rowi[r0:r0 + ns] <= t, diff, NEG))
                ws.append(dec * (k_c[r0:r0 + ns] * q_c[t:t + 1, :]))
            att = _dot(jnp.concatenate(ws, axis=0).astype(BF16), e2)
            o_t = jnp.sum(att.reshape(8, ns, GROUP_W) * v_c[r0:r0 + ns][None], axis=1)
            o_s[t0:t0 + 8, :] = o_s[t0:t0 + 8, :] + o_t
        if sb > 0:
            ref = b[r0 - 1:r0, :]
            qt = q_c[r0:r0 + SUB] * jnp.exp(b[r0:r0 + SUB] - ref)
            ke = (k_c[0:r0] * jnp.exp(ref - b[0:r0])).astype(BF16)
            qs = jnp.concatenate([jnp.where(lane_h == h, qt, 0.0) for h in range(N_HEADS)], axis=0)
            full = _dot(_dot_nt(qs.astype(BF16), ke).astype(BF16), vcb[0:r0])
            o_off = jnp.zeros((SUB, GROUP_W), F32)
            for h in range(N_HEADS):
                o_off = o_off + jnp.where(col_h == h, full[h * SUB:(h + 1) * SUB], 0.0)
            o_s[r0:r0 + SUB, :] = o_s[r0:r0 + SUB, :] + o_off
    o_c = o_s[...]
    kt = k_c * jnp.exp(b_last - b)
    d_col = jnp.broadcast_to(jnp.exp(b_last), (LANE, LANE)).T
    s_new = s_prev * jnp.concatenate([d_col, d_col], axis=1) + _dot(kt.T.astype(BF16), v_c.astype(BF16)) * bd
    s_c[...] = s_new

    lgl = _lane_const(LOG_GAMMA, LANE, LIN_DK)
    rowf = rowi.astype(F32)
    d_prev = s_d[...]
    o_d = _dot((q_d * jnp.exp((rowf + 1.0) * lgl)).astype(BF16), d_prev.astype(BF16))
    lane_h = _iota((BLK, LANE), 1) // LIN_DK
    col_h = _iota((BLK, GROUP_W), 1) // HEAD_V
    rr = _iota((BLK, BLK), 0)
    cc = _iota((BLK, BLK), 1)
    dist = (rr - cc).astype(F32)
    kdb = k_d.astype(BF16)
    vdb = v_d.astype(BF16)
    for h in range(N_HEADS):
        qm = jnp.where(lane_h == h, q_d, 0.0).astype(BF16)
        dec = jnp.exp(jnp.where(rr >= cc, dist * LOG_GAMMA[h], NEG))
        att = (_dot_nt(qm, kdb) * dec).astype(BF16)
        o_d = o_d + jnp.where(col_h == h, _dot(att, vdb), 0.0)
    deck = jnp.where(valid, jnp.exp((n_v - 1.0 - rowf) * lgl), 0.0)
    lgr = jnp.full((LANE, 1), LOG_GAMMA[-1], F32)
    rrow = _iota((LANE, 1), 0) // LIN_DK
    for h in range(N_HEADS - 2, -1, -1):
        lgr = jnp.where(rrow == h, LOG_GAMMA[h], lgr)
    d_new = d_prev * jnp.exp(n_v * lgr) + _dot((k_d * deck).T.astype(BF16), vdb) * bd
    s_d[...] = d_new

    mavg = _seg_mean_matrix(GROUP_W, HEAD_V)
    gout = gout_ref[...]
    cat_ref[0, :, 0:GROUP_W] = (_silu(r_c) * _head_rms(o_c, mavg, gout[:, 512:768])).astype(cat_ref.dtype)
    cat_ref[0, :, GROUP_W:2 * GROUP_W] = (_silu(r_d) * _head_rms(o_d, mavg, gout[:, 768:1024])).astype(cat_ref.dtype)

    @pl.when(c == nchunks - 1)
    def _():
        sc_ref[0] = _state_rows(s_new)
        sd_ref[0] = _state_rows(d_new)


def _lin(cd, lg, sc0, sd0, g_out_l, valid_len, cat_dtype):
    b, lp, _ = cd.shape
    nchunks = lp // BLK
    n_tiles = BLK // 8 if nchunks > 1 else -(-valid_len // 8)
    st = pl.BlockSpec((1, LANE, HEAD_V), lambda i, j: (i, 0, 0))
    return pl.pallas_call(
        functools.partial(_lin_kernel, valid_len=valid_len, n_tiles=n_tiles, nchunks=nchunks),
        grid=(b, nchunks),
        in_specs=[pl.BlockSpec((1, BLK, W_CD), lambda i, j: (i, j, 0)),
                  pl.BlockSpec((1, BLK, LANE), lambda i, j: (i, j, 1)),
                  st, st, _const_spec((1, 1024))],
        out_specs=[pl.BlockSpec((1, BLK, 2 * GROUP_W), lambda i, j: (i, j, 0)), st, st],
        out_shape=[jax.ShapeDtypeStruct((b, lp, 2 * GROUP_W), cat_dtype),
                   jax.ShapeDtypeStruct((b, LANE, HEAD_V), F32),
                   jax.ShapeDtypeStruct((b, LANE, HEAD_V), F32)],
        scratch_shapes=[pltpu.VMEM((LANE, GROUP_W), F32), pltpu.VMEM((LANE, GROUP_W), F32),
                        pltpu.VMEM((BLK, GROUP_W), F32)],
        compiler_params=_params(("parallel", "arbitrary")),
        name="lin",
    )(cd, lg, sc0, sd0, g_out_l)


FF_CHUNK = 256


def _post_kernel(x_ref, cab_ref, ccd_ref, wo_ref, g1_ref, b1_ref, wfi_ref, wfo_ref, g2_ref, b2_ref, o_ref,
                 act_ref, *, alpha):
    y = (_dot(cab_ref[...].astype(BF16), wo_ref[0:512, :])
         + _dot(ccd_ref[...].astype(BF16), wo_ref[512:1024, :]))
    x1 = _layer_norm(alpha * x_ref[...] + y, g1_ref[...], b1_ref[...])
    x1b = x1.astype(BF16)
    for j in range(D_FF // FF_CHUNK):
        gate = _dot(x1b, wfi_ref[:, j * FF_CHUNK:(j + 1) * FF_CHUNK])
        up = _dot(x1b, wfi_ref[:, D_FF + j * FF_CHUNK:D_FF + (j + 1) * FF_CHUNK])
        act_ref[:, j * FF_CHUNK:(j + 1) * FF_CHUNK] = (_silu(gate) * up).astype(BF16)
    y2 = _dot(act_ref[...], wfo_ref[...])
    o_ref[...] = _layer_norm(alpha * x1 + y2, g2_ref[...], b2_ref[...])


def _post(x, cab, ccd, wo, g1, b1, wfi, wfo, g2, b2, tm, alpha):
    t = x.shape[0]
    row = lambda n: pl.BlockSpec((tm, n), lambda i: (i, 0))
    vec = _const_spec((1, D_MODEL))
    return pl.pallas_call(
        functools.partial(_post_kernel, alpha=alpha),
        grid=(t // tm,),
        in_specs=[row(D_MODEL), row(512), row(512), _const_spec((D_MODEL, D_MODEL)), vec, vec,
                  _const_spec((D_MODEL, 2 * D_FF)), _const_spec((D_FF, D_MODEL)), vec, vec],
        out_specs=row(D_MODEL),
        out_shape=jax.ShapeDtypeStruct((t, D_MODEL), F32),
        scratch_shapes=[pltpu.VMEM((tm, D_FF), BF16)],
        compiler_params=_params(("parallel",)),
        name="post",
    )(x, cab, ccd, wo, g1, b1, wfi, wfo, g2, b2)


PAGES_PER_STEP = 16
NQ = 8


def _rep_rows(x4):
    n = x4.shape[1]
    return jnp.concatenate([jnp.broadcast_to(x4[h:h + 1, :], (NQ, n)) for h in range(N_HEADS)], axis=0)


def _dec_kernel(pt_ref, *refs, lam_init, past_len, n_steps):
    g_pages = PAGES_PER_STEP
    ka_refs = refs[0:g_pages]
    va_refs = refs[g_pages:2 * g_pages]
    kb_refs = refs[2 * g_pages:3 * g_pages]
    vb_refs = refs[3 * g_pages:4 * g_pages]
    lf_refs = refs[4 * g_pages:5 * g_pages]
    (q_ref, kvn_ref, lfr_ref, lfc_ref, lp_ref, g_ref, o_ref,
     qa_s, qb_s, m_s, l_s, acc_s, carry_s, nq_s) = refs[5 * g_pages:]
    step = pl.program_id(1)
    n_pages = n_steps * g_pages
    width = g_pages * PAGE

    rows_a = 2 * N_HEADS * NQ
    rows_b = N_HEADS * NQ
    ra = _iota((rows_a, 1), 0)
    slope_a = jnp.full((rows_a, 1), SLOPES[-1], F32)
    for h in range(N_HEADS - 2, -1, -1):
        slope_a = jnp.where(ra // (2 * NQ) == h, SLOPES[h], slope_a)
    t_a = (past_len + ra % NQ).astype(F32)

    @pl.when(step == 0)
    def _():
        q = q_ref[...]
        qa = jnp.concatenate([q[:, 0:GROUP_W]] * (2 * N_HEADS), axis=0)
        r = _iota((rows_a, GROUP_W), 0)
        cidx = _iota((rows_a, GROUP_W), 1)
        keep = (cidx // HEAD_V == r // (2 * NQ)) & ((cidx % HEAD_V) // DIFF_DQK == (r // NQ) % 2)
        qa_s[...] = jnp.where(keep, qa, 0.0).astype(BF16)
        qb = jnp.concatenate([q[:, 768:1024]] * N_HEADS, axis=0)
        r = _iota((rows_b, GROUP_W), 0)
        cidx = _iota((rows_b, GROUP_W), 1)
        qb_s[...] = jnp.where(cidx // HEAD_V == r // NQ, qb, 0.0).astype(BF16)
        m_s[...] = jnp.full(m_s.shape, NEG, F32)
        l_s[...] = jnp.zeros(l_s.shape, F32)
        acc_s[...] = jnp.zeros(acc_s.shape, F32)
        carry_s[...] = jnp.zeros(carry_s.shape, F32)
        cum = _dot_rhs3(_tri_incl(NQ), lfc_ref[0])
        nq_s[...] = jnp.concatenate(
            [jnp.broadcast_to(cum[:, h:h + 1], (NQ, LANE)) for h in range(N_HEADS)], axis=0)

    def update(s, va_t, vb_t, nt):
        m_prev = m_s[...]
        m_new = jnp.maximum(m_prev, jnp.max(s, axis=1, keepdims=True))
        alpha = jnp.exp(m_prev - m_new)
        p = jnp.exp(s - m_new)
        l_s[...] = alpha * l_s[...] + jnp.sum(p, axis=1, keepdims=True)
        pb = p.astype(BF16)
        if nt:
            pv_a = _dot_nt(pb[0:rows_a], va_t)
            pv_b = _dot_nt(pb[rows_a:], vb_t)
        else:
            pv_a = _dot(pb[0:rows_a], va_t)
            pv_b = _dot(pb[rows_a:], vb_t)
        acc_s[...] = alpha * acc_s[...] + jnp.concatenate([pv_a, pv_b], axis=0)
        m_s[...] = m_new

    pages = lambda rs: jnp.concatenate([r[0, 0].reshape(GROUP_W, PAGE).astype(BF16) for r in rs], axis=1)
    lane = _iota((1, width), 1)
    pos = ((n_pages - 1 - step * g_pages - lane // PAGE) * PAGE + lane % PAGE).astype(F32)
    strict = jnp.where(_iota((LANE, LANE), 0) > _iota((LANE, LANE), 1), 1.0, 0.0).astype(BF16)
    carry = carry_s[...]
    parts = []
    for r in range(g_pages):
        lf = jnp.concatenate([lf_refs[r][0, 0], jnp.zeros((8 - N_HEADS, PAGE), F32)], axis=0)
        parts.append(_dot_lhs3(lf, strict) + carry)
        carry = carry + jnp.sum(lf, axis=1, keepdims=True)
    carry_s[...] = carry
    csuf = jnp.concatenate(parts, axis=1)
    nq = jnp.concatenate([nq_s[...]] * g_pages, axis=1)
    s_a = _dot(qa_s[...], pages(ka_refs)) - slope_a * (t_a - pos)
    s_b = _dot(qb_s[...], pages(kb_refs)) + nq + _rep_rows(csuf)
    update(jnp.concatenate([s_a, s_b], axis=0), pages(va_refs), pages(vb_refs), True)

    @pl.when(step == n_steps - 1)
    def _():
        kvn = kvn_ref[...]
        zpad = jnp.zeros((PAGE - NQ, GROUP_W), F32)
        kn_a = jnp.concatenate([kvn[:, 0:256], zpad], axis=0).astype(BF16)
        vn_a = jnp.concatenate([kvn[:, 256:512], zpad], axis=0).astype(BF16)
        kn_b = jnp.concatenate([kvn[:, 512:768], zpad], axis=0).astype(BF16)
        vn_b = jnp.concatenate([kvn[:, 768:1024], zpad], axis=0).astype(BF16)
        jl = _iota((1, LANE), 1)
        ok_a = jl <= ra % NQ
        rb = _iota((rows_b, 1), 0)
        ok_b = jl <= rb % NQ
        cum_row = _dot_lhs3(lfr_ref[0], jnp.where(_iota((LANE, LANE), 0) <= _iota((LANE, LANE), 1),
                                                  1.0, 0.0).astype(BF16))
        sn_a = _dot_nt(qa_s[...], kn_a) - slope_a * ((ra % NQ).astype(F32) - jl.astype(F32))
        sn_b = _dot_nt(qb_s[...], kn_b) + nq_s[...] - _rep_rows(cum_row)
        update(jnp.concatenate([jnp.where(ok_a, sn_a, NEG), jnp.where(ok_b, sn_b, NEG)], axis=0),
               vn_a, vn_b, False)

        lam = _lam(lp_ref, lam_init)
        acc = acc_s[...] / l_s[...]
        col_h = _iota((NQ, GROUP_W), 1) // HEAD_V
        o_a = jnp.zeros((NQ, GROUP_W), F32)
        o_b = jnp.zeros((NQ, GROUP_W), F32)
        for h in range(N_HEADS):
            r0 = 2 * NQ * h
            o_a = o_a + jnp.where(col_h == h, acc[r0:r0 + NQ] - lam * acc[r0 + NQ:r0 + 2 * NQ], 0.0)
            rb0 = rows_a + NQ * h
            o_b = o_b + jnp.where(col_h == h, acc[rb0:rb0 + NQ], 0.0)
        mavg = _seg_mean_matrix(GROUP_W, HEAD_V)
        g = g_ref[...]
        o_ref[:, 0:GROUP_W] = _head_rms(o_a, mavg, g[:, 0:256] * (1.0 - lam_init))
        o_ref[:, GROUP_W:2 * GROUP_W] = _head_rms(o_b, mavg, g[:, 256:512])


def _attn_sample(page_table, caches, layer, qkv, kvn, lf_row, lf_col, lam_params_l, g_out_l, lam_init):
    ck_a, cv_a, ck_b, cv_b, c_lf = caches
    nb, n_pages = page_table.shape
    g_pages = PAGES_PER_STEP
    assert n_pages % g_pages == 0
    n_steps = n_pages // g_pages
    past_len = n_pages * PAGE

    def page_spec(r, lf=False):
        def imap(i, s, pt):
            return (layer, pt[i, n_pages - 1 - (s * g_pages + r)], 0, 0) + (() if lf else (0,))
        return pl.BlockSpec((1, 1, N_HEADS, PAGE) if lf else (1, 1, N_HEADS, HEAD_V, PAGE), imap)

    tok = lambda n: pl.BlockSpec((NQ, n), lambda i, s, pt: (i, 0))
    small = lambda: pl.BlockSpec((1, 8, LANE), lambda i, s, pt: (i, 0, 0))
    in_specs = ([page_spec(r) for r in range(g_pages)] * 4 + [page_spec(r, True) for r in range(g_pages)]
                + [tok(W_AB), tok(1024), small(), small(),
                   pl.BlockSpec((4, DIFF_DQK), lambda i, s, pt: (0, 0)),
                   pl.BlockSpec((1, 1024), lambda i, s, pt: (0, 0))])
    rows = 3 * N_HEADS * NQ
    return pl.pallas_call(
        functools.partial(_dec_kernel, lam_init=lam_init, past_len=past_len, n_steps=n_steps),
        grid_spec=pltpu.PrefetchScalarGridSpec(
            num_scalar_prefetch=1,
            grid=(nb, n_steps),
            in_specs=in_specs,
            out_specs=pl.BlockSpec((NQ, 2 * GROUP_W), lambda i, s, pt: (i, 0)),
            scratch_shapes=[pltpu.VMEM((2 * N_HEADS * NQ, GROUP_W), BF16),
                            pltpu.VMEM((N_HEADS * NQ, GROUP_W), BF16),
                            pltpu.VMEM((rows, 1), F32), pltpu.VMEM((rows, 1), F32),
                            pltpu.VMEM((rows, GROUP_W), F32),
                            pltpu.VMEM((8, 1), F32), pltpu.VMEM((N_HEADS * NQ, LANE), F32)]),
        out_shape=jax.ShapeDtypeStruct((nb * NQ, 2 * GROUP_W), F32),
        compiler_params=_params(("parallel", "arbitrary")),
        name="attn_sample",
    )(page_table, *([ck_a] * g_pages + [cv_a] * g_pages + [ck_b] * g_pages + [cv_b] * g_pages
                    + [c_lf] * g_pages), qkv, kvn, lf_row, lf_col, lam_params_l, g_out_l)


def kernel(x_prompt, x_sample, cache_a_k, cache_a_v, cache_b_k, cache_b_v, cache_b_logf, state_c, state_d,
           page_table, meta_tokens, w_in, b_f, w_c_gate, b_c_gate, lam_params, g_out, w_out, ln1_g, ln1_b,
           w_ffn_in, w_ffn_out, ln2_g, ln2_b):
    depth = w_in.shape[0]
    alpha = (2.0 * depth) ** 0.25
    nb, seq, _ = x_prompt.shape
    db, nq, _ = x_sample.shape
    assert nq == NQ
    l_true = N_META + seq
    lp = -(-l_true // BLK) * BLK
    xp = jnp.concatenate([jnp.broadcast_to(meta_tokens.astype(F32), (nb, N_META, D_MODEL)), x_prompt,
                          jnp.zeros((nb, lp - l_true, D_MODEL), F32)], axis=1).reshape(nb * lp, D_MODEL)
    xs = x_sample.reshape(db * nq, D_MODEL)
    tm_p = 512 if (nb * lp) % 512 == 0 else BLK
    tm_s = db * nq

    tr = lambda c: jnp.transpose(c, (0, 1, 3, 4, 2))
    caches = (tr(cache_a_k), tr(cache_a_v), tr(cache_b_k), tr(cache_b_v), jnp.transpose(cache_b_logf, (0, 1, 3, 2)))

    w_all = _prep_w_in(jnp.transpose(w_in, (0, 2, 1)))
    kvs, lgs, p_states, s_rows = [], [], [], []
    for l in range(depth):
        lam_init = 0.8 - 0.6 * math.exp(-0.3 * l)
        w = w_all[l]
        bf = jnp.pad(b_f[l], (0, LANE - N_HEADS)).reshape(1, LANE)
        wcg = jnp.pad(w_c_gate[l], ((0, LANE - GLA_RANK), (0, 0)))
        bcg = b_c_gate[l].reshape(1, LANE)
        gl = g_out[l].reshape(1, 1024)
        wo = w_out[l].astype(BF16)
        wfi = w_ffn_in[l].astype(BF16)
        wfo = w_ffn_out[l].astype(BF16)
        vec = lambda a: a[l].reshape(1, D_MODEL)
        post = functools.partial(_post, wo=wo, g1=vec(ln1_g), b1=vec(ln1_b), wfi=wfi, wfo=wfo,
                                 g2=vec(ln2_g), b2=vec(ln2_b), alpha=alpha)

        qkv, kv, cd, lg, vt = _proj(xp, w, bf, wcg, bcg, tm_p, BF16, LOG2E)
        lg3 = lg.reshape(nb, lp, 2 * LANE)
        kvs.append(kv.reshape(nb, lp, 1024))
        lgs.append(lg3)
        cab = _attn_prompt(qkv.reshape(nb, lp, W_AB), vt, lg3, lam_params[l], gl, lam_init)
        zero_state = jnp.zeros((nb, LANE, HEAD_V), F32)
        ccd, sc_p, sd_p = _lin(cd.reshape(nb, lp, W_CD), lg3, zero_state, zero_state, gl, l_true, BF16)
        xp = post(xp, cab.reshape(nb * lp, 512), ccd.reshape(nb * lp, 512), tm=tm_p)
        p_states.append((sc_p.reshape(nb, N_HEADS, LIN_DK, HEAD_V), sd_p.reshape(nb, N_HEADS, LIN_DK, HEAD_V)))

        qkv_s, kv_s, cd_s, lg_s, _ = _proj(xs, w, bf, wcg, bcg, tm_s, F32, 1.0)
        lf_s = lg_s[:, 0:N_HEADS].reshape(db, nq, N_HEADS)
        lf_col = jnp.pad(lf_s, ((0, 0), (0, 0), (0, LANE - N_HEADS)))
        lf_row = jnp.pad(jnp.transpose(lf_s, (0, 2, 1)), ((0, 0), (0, 8 - N_HEADS), (0, LANE - nq)))
        cab_s = _attn_sample(page_table, caches, l, qkv_s, kv_s, lf_row, lf_col, lam_params[l], gl, lam_init)
        pad_rows = lambda a: jnp.pad(a.reshape(db, nq, a.shape[-1]), ((0, 0), (0, BLK - nq), (0, 0)))
        ccd_s, sc_s, sd_s = _lin(pad_rows(cd_s), pad_rows(lg_s), state_c[l].reshape(db, LANE, HEAD_V),
                                 state_d[l].reshape(db, LANE, HEAD_V), gl, nq, F32)
        xs = post(xs, cab_s, ccd_s[:, 0:nq].reshape(db * nq, 512), tm=tm_s)
        heads_s = lambda a: a.reshape(db, nq, N_HEADS, HEAD_V)
        s_rows.append((heads_s(kv_s[:, 0:256]), heads_s(kv_s[:, 256:512]), heads_s(kv_s[:, 512:768]),
                       heads_s(kv_s[:, 768:1024]), lf_s,
                       sc_s.reshape(db, N_HEADS, LIN_DK, HEAD_V), sd_s.reshape(db, N_HEADS, LIN_DK, HEAD_V)))

    y_prompt = xp.reshape(nb, lp, D_MODEL)[:, N_META:l_true]
    y_sample = xs.reshape(db, nq, D_MODEL)
    ka_t, va_t, kb_t, vb_t, lf_t = _rows(kvs, lgs, l_true)
    heads_t = lambda a: jnp.transpose(a.reshape(depth, nb, N_HEADS, HEAD_V, l_true), (0, 1, 4, 2, 3))
    p_out = [heads_t(ka_t), heads_t(va_t), heads_t(kb_t), heads_t(vb_t), jnp.transpose(lf_t, (0, 1, 3, 2))]
    p_out += [jnp.stack(z) for z in zip(*p_states)]
    s_out = [jnp.stack(z) for z in zip(*s_rows)]
    return (y_prompt, y_sample, *p_out, *s_out)
```

```python
import functools
import math

import jax
import jax.numpy as jnp
from jax import lax
from jax.experimental import pallas as pl
from jax.experimental.pallas import tpu as pltpu

F32 = jnp.float32
BF16 = jnp.bfloat16

D_MODEL = 1024
N_META = 16
N_HEADS = 4
HEAD_V = 64
GROUP_W = 256
DIFF_DQK = 32
LIN_DK = 32
GLA_RANK = 16
GLA_TAU = 16.0
D_FF = 2816
EPS = 1e-5
NEG = -1e30
PAGE = 128
BLK = 128
LANE = 128
A_SCALE = DIFF_DQK ** -0.5
B_SCALE = HEAD_V ** -0.5
LIN_SCALE = LIN_DK ** -0.5
SLOPES = tuple(2.0 ** (-8.0 * (h + 1.0) / N_HEADS) for h in range(N_HEADS))
LOG_GAMMA = tuple(math.log1p(-(2.0 ** (-5.0 - h))) for h in range(N_HEADS))
LOG2E = math.log2(math.e)
IN_SIZES = (256, 256, 256, 256, 256, 256, 4, 128, 128, 256, 256, 16, 128, 128, 256, 256)
VMEM_LIMIT = 56 * 1024 * 1024


def _dot(a, b):
    return jnp.dot(a, b, preferred_element_type=F32)


def _dot_nt(a, b):
    return lax.dot_general(a, b, (((1,), (1,)), ((), ())), preferred_element_type=F32)


def _split3(x):
    hi = x.astype(BF16)
    r = x - hi.astype(F32)
    mid = r.astype(BF16)
    lo = (r - mid.astype(F32)).astype(BF16)
    return hi, mid, lo


def _dot_lhs3(x, m):
    hi, mid, lo = _split3(x)
    return _dot(hi, m) + _dot(mid, m) + _dot(lo, m)


def _dot_rhs3(m, x):
    hi, mid, lo = _split3(x)
    return _dot(m, hi) + _dot(m, mid) + _dot(m, lo)


def _iota(shape, dim):
    return lax.broadcasted_iota(jnp.int32, shape, dim)


def _tri_incl(n):
    return jnp.where(_iota((n, n), 0) >= _iota((n, n), 1), 1.0, 0.0).astype(BF16)


def _seg_mean_matrix(width, seg):
    same = (_iota((width, width), 0) // seg) == (_iota((width, width), 1) // seg)
    return jnp.where(same, 1.0 / seg, 0.0).astype(BF16)


def _block_diag_mask():
    r = _iota((LANE, GROUP_W), 0) // LIN_DK
    c = _iota((LANE, GROUP_W), 1) // HEAD_V
    return jnp.where(r == c, 1.0, 0.0).astype(F32)


def _log_sigmoid(x):
    return jnp.minimum(x, 0.0) - jnp.log1p(jnp.exp(-jnp.abs(x)))


def _silu(x):
    return x / (1.0 + jnp.exp(-x))


def _head_rms(o, mavg, gain):
    ms = _dot_lhs3(o * o, mavg)
    return o * lax.rsqrt(ms + EPS) * gain


def _layer_norm(x, g, b):
    mu = jnp.mean(x, axis=-1, keepdims=True)
    xc = x - mu
    var = jnp.mean(xc * xc, axis=-1, keepdims=True)
    return xc * lax.rsqrt(var + EPS) * g + b


def _lam(lp_ref, lam_init):
    lp = lp_ref[...]
    a = jnp.sum(lp[0:1, :] * lp[1:2, :], axis=1, keepdims=True)
    b = jnp.sum(lp[2:3, :] * lp[3:4, :], axis=1, keepdims=True)
    return jnp.exp(a) - jnp.exp(b) + lam_init


def _params(sem):
    return pltpu.CompilerParams(dimension_semantics=sem, vmem_limit_bytes=VMEM_LIMIT)


def _const_spec(shape):
    n = len(shape)
    return pl.BlockSpec(shape, lambda *_: (0,) * n, pipeline_mode=pl.Buffered(1))


W_AB = 1536
W_CD = 1536
W_FG = 256
W_ALL = W_AB + W_CD + W_FG


def _prep_w_in(w_in_t):
    offs = [0]
    for s in IN_SIZES:
        offs.append(offs[-1] + s)
    seg = lambda i: w_in_t[:, offs[i]:offs[i + 1], :]
    pad = lambda a: jnp.pad(a, ((0, 0), (0, LANE - a.shape[1]), (0, 0)))
    rows = [seg(i) for i in (0, 1, 2, 3, 4, 5)] + [seg(i) for i in (7, 8, 9, 10, 12, 13, 14, 15)]
    rows += [pad(seg(6)), pad(seg(11))]
    return jnp.concatenate(rows, axis=1).astype(BF16)


def _proj_kernel(x_ref, w_ref, bf_ref, wcg_ref, bcg_ref, qkv_ref, kv_ref, cd_ref, lg_ref, vt_ref, *, q_scale):
    xb = x_ref[...].astype(BF16)
    ab = _dot_nt(xb, w_ref[0:W_AB, :])
    dt = qkv_ref.dtype
    qkv_ref[:, 0:256] = (ab[:, 0:256] * (A_SCALE * q_scale)).astype(dt)
    qkv_ref[:, 256:768] = ab[:, 256:768].astype(dt)
    qkv_ref[:, 768:1024] = (ab[:, 768:1024] * (B_SCALE * q_scale)).astype(dt)
    qkv_ref[:, 1024:1536] = ab[:, 1024:1536].astype(dt)
    kv_ref[:, 0:512] = ab[:, 256:768]
    kv_ref[:, 512:1024] = ab[:, 1024:1536]
    vt_ref[0:GROUP_W, :] = ab[:, 512:768].T.astype(BF16)
    vt_ref[GROUP_W:2 * GROUP_W, :] = ab[:, 1280:1536].T.astype(BF16)
    cd_ref[...] = _dot_nt(xb, w_ref[W_AB:W_AB + W_CD, :])
    fa = _dot_nt(xb, w_ref[W_AB + W_CD:W_ALL, :])
    lg_ref[:, 0:LANE] = _log_sigmoid(fa[:, 0:LANE] + bf_ref[...])
    a = fa[:, LANE:2 * LANE]
    a_hi = a.astype(BF16)
    a_lo = (a - a_hi.astype(F32)).astype(BF16)
    w = wcg_ref[...]
    w_hi = w.astype(BF16)
    w_lo = (w - w_hi.astype(F32)).astype(BF16)
    z = _dot(a_hi, w_hi) + _dot(a_hi, w_lo) + _dot(a_lo, w_hi) + bcg_ref[...]
    lg_ref[:, LANE:2 * LANE] = _log_sigmoid(z) / GLA_TAU


def _proj(x, w, bf, wcg, bcg, tm, qkv_dtype, q_scale):
    t = x.shape[0]
    row = lambda n: pl.BlockSpec((tm, n), lambda i: (i, 0))
    return pl.pallas_call(
        functools.partial(_proj_kernel, q_scale=q_scale),
        grid=(t // tm,),
        in_specs=[row(D_MODEL), _const_spec((W_ALL, D_MODEL)), _const_spec((1, LANE)),
                  _const_spec((LANE, LANE)), _const_spec((1, LANE))],
        out_specs=[row(W_AB), row(1024), row(W_CD), row(2 * LANE),
                   pl.BlockSpec((2 * GROUP_W, tm), lambda i: (0, i))],
        out_shape=[jax.ShapeDtypeStruct((t, W_AB), qkv_dtype),
                   jax.ShapeDtypeStruct((t, 1024), F32),
                   jax.ShapeDtypeStruct((t, W_CD), F32),
                   jax.ShapeDtypeStruct((t, 2 * LANE), F32),
                   jax.ShapeDtypeStruct((2 * GROUP_W, t), BF16)],
        compiler_params=_params(("parallel",)),
        name="proj",
    )(x, w, bf, wcg, bcg)


ROWS_BLK = 512


def _rows_kernel(*refs, depth):
    kv_refs = refs[0:depth]
    lg_refs = refs[depth:2 * depth]
    ka_ref, va_ref, kb_ref, vb_ref, lf_ref = refs[2 * depth:]
    for l in range(depth):
        kv = kv_refs[l][0]
        ka_ref[l, 0] = kv[:, 0:256].T
        va_ref[l, 0] = kv[:, 256:512].T
        kb_ref[l, 0] = kv[:, 512:768].T
        vb_ref[l, 0] = kv[:, 768:1024].T
        lf_ref[l, 0] = lg_refs[l][0].T[0:N_HEADS, :]


def _rows(kvs, lgs, l_true):
    depth = len(kvs)
    b, lp, _ = kvs[0].shape
    rb = min(ROWS_BLK, lp)
    out = pl.BlockSpec((depth, 1, GROUP_W, rb), lambda i, j: (0, i, 0, j))
    return pl.pallas_call(
        functools.partial(_rows_kernel, depth=depth),
        grid=(b, pl.cdiv(l_true, rb)),
        in_specs=[pl.BlockSpec((1, rb, 1024), lambda i, j: (i, j, 0))] * depth
        + [pl.BlockSpec((1, rb, LANE), lambda i, j: (i, j, 0))] * depth,
        out_specs=[out, out, out, out, pl.BlockSpec((depth, 1, N_HEADS, rb), lambda i, j: (0, i, 0, j))],
        out_shape=[jax.ShapeDtypeStruct((depth, b, GROUP_W, l_true), F32)] * 4
        + [jax.ShapeDtypeStruct((depth, b, N_HEADS, l_true), F32)],
        compiler_params=_params(("parallel", "parallel")),
        name="rows",
    )(*kvs, *lgs)


BIAS_F0 = 3 * N_HEADS


A_W = 4 * BLK
B_W = 2 * BLK
S_W = 2 * A_W + 2 * B_W


def _map_lanes(h, m):
    j, half = divmod(h, 2)
    if m < 2:
        return j * A_W + (2 * half + m) * BLK
    return 2 * A_W + j * B_W + half * BLK


def _attn_kernel(qa_ref, ka_ref, qb_ref, kb_ref, vt_ref, lf_ref, lp_ref, g_ref, o_ref,
                 kbias, wqa, wqb, m_ref, l_ref, acc_ref, s_even, s_odd, *, lam_init, nblk):
    qi = pl.program_id(1)
    row = _iota((BLK, BLK), 0)
    col = _iota((BLK, BLK), 1)

    @pl.when(qi == 0)
    def _():
        tri = _tri_incl(BLK)
        sel = [[jnp.where((row < N_HEADS) & (col == base + 3 * row + i), 1.0, 0.0).astype(BF16)
                for i in range(3)] for base in (0, BIAS_F0)]
        slope = _lane_const(tuple(LOG2E * s for s in SLOPES) + (0.0,), LANE, 1)
        carry = jnp.zeros((1, LANE), F32)
        for i in range(nblk):
            cs = _dot_rhs3(tri, lf_ref[0, i * BLK:(i + 1) * BLK, :]) + carry
            carry = cs[BLK - 1:BLK, :]
            pos = (i * BLK + row).astype(F32)
            acc = jnp.zeros((BLK, LANE), F32)
            for terms, mats in ((_split3(slope * pos), sel[0]), (_split3(cs * -LOG2E), sel[1])):
                for t, mat in zip(terms, mats):
                    acc = acc + _dot(t, mat)
            kbias[i * BLK:(i + 1) * BLK, :] = acc.astype(BF16)

    m_ref[...] = jnp.full(m_ref.shape, NEG, F32)
    l_ref[...] = jnp.zeros(l_ref.shape, F32)
    acc_ref[...] = jnp.zeros(acc_ref.shape, F32)

    qa = qa_ref[0].astype(F32)
    qb = qb_ref[0].astype(F32)
    for j in range(2):
        qa_t = qa[:, j * LANE:(j + 1) * LANE].T
        qb_t = qb[:, j * LANE:(j + 1) * LANE].T
        for half in range(2):
            h = 2 * j + half
            lo = half * HEAD_V
            w_alibi = jnp.where((row >= 3 * h) & (row < 3 * h + 3), 1.0, 0.0).astype(BF16)
            w_fox = jnp.where((row >= BIAS_F0 + 3 * h) & (row < BIAS_F0 + 3 * h + 3), 1.0, 0.0).astype(BF16)
            for m in range(2):
                keep = (row >= lo + m * DIFF_DQK) & (row < lo + (m + 1) * DIFF_DQK)
                c0 = (2 * half + m) * BLK
                wqa[j, 0:LANE, c0:c0 + BLK] = jnp.where(keep, qa_t, 0.0).astype(BF16)
                wqa[j, LANE:2 * LANE, c0:c0 + BLK] = w_alibi
            keep = (row >= lo) & (row < lo + HEAD_V)
            wqb[j, 0:LANE, half * BLK:(half + 1) * BLK] = jnp.where(keep, qb_t, 0.0).astype(BF16)
            wqb[j, LANE:2 * LANE, half * BLK:(half + 1) * BLK] = w_fox

    def scores(kj):
        ks = pl.multiple_of(kj * BLK, BLK)
        bias = kbias[pl.ds(ks, BLK), :]
        parts = []
        for k_ref, w_ref in ((ka_ref, wqa), (kb_ref, wqb)):
            for j in range(2):
                kk = jnp.concatenate([k_ref[0, pl.ds(ks, BLK), j * LANE:(j + 1) * LANE], bias], axis=1)
                parts.append(_dot(kk, w_ref[j]))
        return jnp.concatenate(parts, axis=1)

    def update(kj, s):
        ks = pl.multiple_of(kj * BLK, BLK)
        m_prev = m_ref[...]
        m_new = jnp.maximum(m_prev, jnp.max(s, axis=0, keepdims=True))
        alpha = jnp.exp2(m_prev - m_new)
        p = jnp.exp2(s - m_new)
        l_ref[...] = alpha * l_ref[...] + jnp.sum(p, axis=0, keepdims=True)
        m_ref[...] = m_new
        pb = p.astype(BF16)
        for h in range(N_HEADS):
            vta = vt_ref[h * HEAD_V:(h + 1) * HEAD_V, pl.ds(ks, BLK)]
            vtb = vt_ref[GROUP_W + h * HEAD_V:GROUP_W + (h + 1) * HEAD_V, pl.ds(ks, BLK)]
            for c0, w, vt in ((_map_lanes(h, 0), 2 * BLK, vta), (_map_lanes(h, 2), BLK, vtb)):
                acc_ref[:, c0:c0 + w] = alpha[:, c0:c0 + w] * acc_ref[:, c0:c0 + w] + _dot(vt, pb[:, c0:c0 + w])

    def diag(s):
        return jnp.where(_iota((BLK, S_W), 0) <= _iota((BLK, S_W), 1) % BLK, s, NEG)

    s_even[...] = scores(0)

    def pair(i, carry):
        s_odd[...] = scores(2 * i + 1)
        update(2 * i, s_even[...])
        s_even[...] = scores(2 * i + 2)
        update(2 * i + 1, s_odd[...])
        return carry

    lax.fori_loop(0, qi // 2, pair, 0)

    @pl.when(qi % 2 == 0)
    def _():
        update(qi, diag(s_even[...]))

    @pl.when(qi % 2 == 1)
    def _():
        s_odd[...] = scores(qi)
        update(qi - 1, s_even[...])
        update(qi, diag(s_odd[...]))

    lam = _lam(lp_ref, lam_init)
    g = g_ref[...]
    inv_l = 1.0 / l_ref[...]

    def head_out(h, m):
        c0 = _map_lanes(h, m)
        return acc_ref[:, c0:c0 + BLK] * inv_l[:, c0:c0 + BLK]

    def rms_t(o):
        return o * lax.rsqrt(jnp.mean(o * o, axis=0, keepdims=True) + EPS)

    for j in range(2):
        oa, ob = [], []
        for half in range(2):
            h = 2 * j + half
            oa.append(rms_t(head_out(h, 0) - lam * head_out(h, 1)))
            ob.append(rms_t(head_out(h, 2)))
        ga = g[:, j * LANE:(j + 1) * LANE] * (1.0 - lam_init)
        gb = g[:, GROUP_W + j * LANE:GROUP_W + (j + 1) * LANE]
        o_ref[0, :, j * LANE:(j + 1) * LANE] = (jnp.concatenate(oa, axis=0).T * ga).astype(o_ref.dtype)
        o_ref[0, :, GROUP_W + j * LANE:GROUP_W + (j + 1) * LANE] = (
            jnp.concatenate(ob, axis=0).T * gb).astype(o_ref.dtype)


def _attn_prompt(qkv, vt, lg, lam_params_l, g_out_l, lam_init):
    b, lp, _ = qkv.shape
    nblk = lp // BLK
    assert nblk <= 256
    qspec = lambda c: pl.BlockSpec((1, BLK, GROUP_W), lambda i, j: (i, j, c))
    kspec = lambda c: pl.BlockSpec((1, lp, GROUP_W), lambda i, j: (i, 0, c))
    return pl.pallas_call(
        functools.partial(_attn_kernel, lam_init=lam_init, nblk=nblk),
        grid=(b, nblk),
        in_specs=[qspec(0), kspec(1), qspec(3), kspec(4),
                  pl.BlockSpec((2 * GROUP_W, lp), lambda i, j: (0, i)),
                  pl.BlockSpec((1, lp, LANE), lambda i, j: (i, 0, 0)),
                  _const_spec((4, DIFF_DQK)), _const_spec((1, 1024))],
        out_specs=pl.BlockSpec((1, BLK, 2 * GROUP_W), lambda i, j: (i, j, 0)),
        out_shape=jax.ShapeDtypeStruct((b, lp, 2 * GROUP_W), BF16),
        scratch_shapes=[pltpu.VMEM((lp, LANE), BF16),
                        pltpu.VMEM((2, 2 * LANE, A_W), BF16), pltpu.VMEM((2, 2 * LANE, B_W), BF16),
                        pltpu.VMEM((1, S_W), F32), pltpu.VMEM((1, S_W), F32), pltpu.VMEM((HEAD_V, S_W), F32),
                        pltpu.VMEM((BLK, S_W), F32), pltpu.VMEM((BLK, S_W), F32)],
        compiler_params=_params(("parallel", "arbitrary")),
        name="attn_prompt",
    )(qkv, qkv, qkv, qkv, vt, lg, lam_params_l, g_out_l)


SUB = 32


def _lane_const(values, width, seg):
    lane = _iota((1, width), 1) // seg
    out = jnp.full((1, width), values[-1], F32)
    for h in range(len(values) - 2, -1, -1):
        out = jnp.where(lane == h, values[h], out)
    return out


def _state_rows(s_bd):
    hsum = s_bd[:, 0:LANE] + s_bd[:, LANE:2 * LANE]
    return (hsum + pltpu.roll(hsum, HEAD_V, 1))[:, 0:HEAD_V]


def _lin_kernel(cd_ref, g_ref, sc0_ref, sd0_ref, gout_ref, cat_ref, sc_ref, sd_ref, s_c, s_d, o_s,
                *, valid_len, n_tiles, nchunks):
    c = pl.program_id(1)
    bd = _block_diag_mask()
    e2 = bd.astype(BF16)

    @pl.when(c == 0)
    def _():
        expand = jnp.where(_iota((HEAD_V, GROUP_W), 0) == _iota((HEAD_V, GROUP_W), 1) % HEAD_V,
                           1.0, 0.0).astype(BF16)
        s_c[...] = _dot_lhs3(sc0_ref[0], expand) * bd
        s_d[...] = _dot_lhs3(sd0_ref[0], expand) * bd

    rowi = _iota((BLK, 1), 0)
    valid = (c * BLK + rowi) < valid_len
    n_v = jnp.clip(valid_len - c * BLK, 0, BLK).astype(F32)
    x = cd_ref[0]
    g = jnp.where(valid, g_ref[0], 0.0)
    q_c = x[:, 0:128] * LIN_SCALE
    k_c = jnp.where(valid, x[:, 128:256], 0.0)
    v_c = x[:, 256:512]
    r_c = x[:, 512:768]
    q_d = x[:, 768:896]
    k_d = jnp.where(valid, x[:, 896:1024], 0.0) * LIN_SCALE
    v_d = x[:, 1024:1280]
    r_d = x[:, 1280:1536]

    b = _dot_rhs3(_tri_incl(BLK), g)
    b_last = b[BLK - 1:BLK, :]
    s_prev = s_c[...]
    o_s[...] = _dot((q_c * jnp.exp(b)).astype(BF16), s_prev.astype(BF16))
    lane_h = _iota((SUB, LANE), 1) // LIN_DK
    col_h = _iota((SUB, GROUP_W), 1) // HEAD_V
    vcb = v_c.astype(BF16)
    for sb in range(-(-n_tiles * 8 // SUB)):
        r0 = sb * SUB
        for ti in range(min(SUB // 8, n_tiles - r0 // 8)):
            t0 = r0 + 8 * ti
            ns = 8 * (ti + 1)
            ws = []
            for r in range(8):
                t = t0 + r
                diff = b[t:t + 1, :] - b[r0:r0 + ns, :]
                dec = jnp.exp(jnp.where(rowi[r0:r0 + ns] <= t, diff, NEG))
                ws.append(dec * (k_c[r0:r0 + ns] * q_c[t:t + 1, :]))
            att = _dot(jnp.concatenate(ws, axis=0).astype(BF16), e2)
            o_t = jnp.sum(att.reshape(8, ns, GROUP_W) * v_c[r0:r0 + ns][None], axis=1)
            o_s[t0:t0 + 8, :] = o_s[t0:t0 + 8, :] + o_t
        if sb > 0:
            ref = b[r0 - 1:r0, :]
            qt = q_c[r0:r0 + SUB] * jnp.exp(b[r0:r0 + SUB] - ref)
            ke = (k_c[0:r0] * jnp.exp(ref - b[0:r0])).astype(BF16)
            qs = jnp.concatenate([jnp.where(lane_h == h, qt, 0.0) for h in range(N_HEADS)], axis=0)
            full = _dot(_dot_nt(qs.astype(BF16), ke).astype(BF16), vcb[0:r0])
            o_off = jnp.zeros((SUB, GROUP_W), F32)
            for h in range(N_HEADS):
                o_off = o_off + jnp.where(col_h == h, full[h * SUB:(h + 1) * SUB], 0.0)
            o_s[r0:r0 + SUB, :] = o_s[r0:r0 + SUB, :] + o_off
    o_c = o_s[...]
    kt = k_c * jnp.exp(b_last - b)
    d_col = jnp.broadcast_to(jnp.exp(b_last), (LANE, LANE)).T
    s_new = s_prev * jnp.concatenate([d_col, d_col], axis=1) + _dot(kt.T.astype(BF16), v_c.astype(BF16)) * bd
    s_c[...] = s_new

    lgl = _lane_const(LOG_GAMMA, LANE, LIN_DK)
    rowf = rowi.astype(F32)
    d_prev = s_d[...]
    o_d = _dot((q_d * jnp.exp((rowf + 1.0) * lgl)).astype(BF16), d_prev.astype(BF16))
    lane_h = _iota((BLK, LANE), 1) // LIN_DK
    col_h = _iota((BLK, GROUP_W), 1) // HEAD_V
    rr = _iota((BLK, BLK), 0)
    cc = _iota((BLK, BLK), 1)
    dist = (rr - cc).astype(F32)
    kdb = k_d.astype(BF16)
    vdb = v_d.astype(BF16)
    for h in range(N_HEADS):
        qm = jnp.where(lane_h == h, q_d, 0.0).astype(BF16)
        dec = jnp.exp(jnp.where(rr >= cc, dist * LOG_GAMMA[h], NEG))
        att = (_dot_nt(qm, kdb) * dec).astype(BF16)
        o_d = o_d + jnp.where(col_h == h, _dot(att, vdb), 0.0)
    deck = jnp.where(valid, jnp.exp((n_v - 1.0 - rowf) * lgl), 0.0)
    lgr = jnp.full((LANE, 1), LOG_GAMMA[-1], F32)
    rrow = _iota((LANE, 1), 0) // LIN_DK
    for h in range(N_HEADS - 2, -1, -1):
        lgr = jnp.where(rrow == h, LOG_GAMMA[h], lgr)
    d_new = d_prev * jnp.exp(n_v * lgr) + _dot((k_d * deck).T.astype(BF16), vdb) * bd
    s_d[...] = d_new

    mavg = _seg_mean_matrix(GROUP_W, HEAD_V)
    gout = gout_ref[...]
    cat_ref[0, :, 0:GROUP_W] = (_silu(r_c) * _head_rms(o_c, mavg, gout[:, 512:768])).astype(cat_ref.dtype)
    cat_ref[0, :, GROUP_W:2 * GROUP_W] = (_silu(r_d) * _head_rms(o_d, mavg, gout[:, 768:1024])).astype(cat_ref.dtype)

    @pl.when(c == nchunks - 1)
    def _():
        sc_ref[0] = _state_rows(s_new)
        sd_ref[0] = _state_rows(d_new)


def _lin(cd, lg, sc0, sd0, g_out_l, valid_len, cat_dtype):
    b, lp, _ = cd.shape
    nchunks = lp // BLK
    n_tiles = BLK // 8 if nchunks > 1 else -(-valid_len // 8)
    st = pl.BlockSpec((1, LANE, HEAD_V), lambda i, j: (i, 0, 0))
    return pl.pallas_call(
        functools.partial(_lin_kernel, valid_len=valid_len, n_tiles=n_tiles, nchunks=nchunks),
        grid=(b, nchunks),
        in_specs=[pl.BlockSpec((1, BLK, W_CD), lambda i, j: (i, j, 0)),
                  pl.BlockSpec((1, BLK, LANE), lambda i, j: (i, j, 1)),
                  st, st, _const_spec((1, 1024))],
        out_specs=[pl.BlockSpec((1, BLK, 2 * GROUP_W), lambda i, j: (i, j, 0)), st, st],
        out_shape=[jax.ShapeDtypeStruct((b, lp, 2 * GROUP_W), cat_dtype),
                   jax.ShapeDtypeStruct((b, LANE, HEAD_V), F32),
                   jax.ShapeDtypeStruct((b, LANE, HEAD_V), F32)],
        scratch_shapes=[pltpu.VMEM((LANE, GROUP_W), F32), pltpu.VMEM((LANE, GROUP_W), F32),
                        pltpu.VMEM((BLK, GROUP_W), F32)],
        compiler_params=_params(("parallel", "arbitrary")),
        name="lin",
    )(cd, lg, sc0, sd0, g_out_l)


FF_CHUNK = 256


def _post_kernel(x_ref, cab_ref, ccd_ref, wo_ref, g1_ref, b1_ref, wfi_ref, wfo_ref, g2_ref, b2_ref, o_ref,
                 act_ref, *, alpha):
    y = (_dot(cab_ref[...].astype(BF16), wo_ref[0:512, :])
         + _dot(ccd_ref[...].astype(BF16), wo_ref[512:1024, :]))
    x1 = _layer_norm(alpha * x_ref[...] + y, g1_ref[...], b1_ref[...])
    x1b = x1.astype(BF16)
    for j in range(D_FF // FF_CHUNK):
        gate = _dot(x1b, wfi_ref[:, j * FF_CHUNK:(j + 1) * FF_CHUNK])
        up = _dot(x1b, wfi_ref[:, D_FF + j * FF_CHUNK:D_FF + (j + 1) * FF_CHUNK])
        act_ref[:, j * FF_CHUNK:(j + 1) * FF_CHUNK] = (_silu(gate) * up).astype(BF16)
    y2 = _dot(act_ref[...], wfo_ref[...])
    o_ref[...] = _layer_norm(alpha * x1 + y2, g2_ref[...], b2_ref[...])


def _post(x, cab, ccd, wo, g1, b1, wfi, wfo, g2, b2, tm, alpha):
    t = x.shape[0]
    row = lambda n: pl.BlockSpec((tm, n), lambda i: (i, 0))
    vec = _const_spec((1, D_MODEL))
    return pl.pallas_call(
        functools.partial(_post_kernel, alpha=alpha),
        grid=(t // tm,),
        in_specs=[row(D_MODEL), row(512), row(512), _const_spec((D_MODEL, D_MODEL)), vec, vec,
                  _const_spec((D_MODEL, 2 * D_FF)), _const_spec((D_FF, D_MODEL)), vec, vec],
        out_specs=row(D_MODEL),
        out_shape=jax.ShapeDtypeStruct((t, D_MODEL), F32),
        scratch_shapes=[pltpu.VMEM((tm, D_FF), BF16)],
        compiler_params=_params(("parallel",)),
        name="post",
    )(x, cab, ccd, wo, g1, b1, wfi, wfo, g2, b2)


PAGES_PER_STEP = 16
NQ = 8


def _rep_rows(x4):
    n = x4.shape[1]
    return jnp.concatenate([jnp.broadcast_to(x4[h:h + 1, :], (NQ, n)) for h in range(N_HEADS)], axis=0)


def _dec_kernel(pt_ref, ka_hbm, va_hbm, kb_hbm, vb_hbm, lf_hbm, q_ref, kvn_ref, lfr_ref, lfc_ref, lp_ref, g_ref,
                o_ref, qa_s, qb_s, m_s, l_s, acc_s, carry_s, nq_s, ka_buf, va_buf, kb_buf, vb_buf, lf_buf, sem,
                *, layer, lam_init, past_len, n_steps):
    g_pages = PAGES_PER_STEP
    seq = pl.program_id(0)
    step = pl.program_id(1)
    n_pages = n_steps * g_pages
    width = g_pages * PAGE

    hbms = (ka_hbm, va_hbm, kb_hbm, vb_hbm, lf_hbm)
    bufs = (ka_buf, va_buf, kb_buf, vb_buf, lf_buf)
    t = seq * n_steps + step
    slot = t % 2

    def page_copies(b_idx, s_idx, sl, known_pages):
        out = []
        for r in range(g_pages):
            page = 0 if known_pages else pt_ref[b_idx, n_pages - 1 - (s_idx * g_pages + r)]
            for c in range(len(hbms)):
                out.append(pltpu.make_async_copy(hbms[c].at[layer, page], bufs[c].at[sl, r], sem.at[sl, c, r]))
        return out

    @pl.when(t == 0)
    def _():
        for cp in page_copies(0, 0, 0, False):
            cp.start()

    @pl.when(t + 1 < pl.num_programs(0) * n_steps)
    def _():
        last = step + 1 == n_steps
        for cp in page_copies(jnp.where(last, seq + 1, seq), jnp.where(last, 0, step + 1), 1 - slot, False):
            cp.start()

    for cp in page_copies(seq, step, slot, True):
        cp.wait()
    ka_refs, va_refs, kb_refs, vb_refs, lf_refs = [[b.at[slot, r] for r in range(g_pages)] for b in bufs]

    rows_a = 2 * N_HEADS * NQ
    rows_b = N_HEADS * NQ
    ra = _iota((rows_a, 1), 0)
    slope_a = jnp.full((rows_a, 1), SLOPES[-1], F32)
    for h in range(N_HEADS - 2, -1, -1):
        slope_a = jnp.where(ra // (2 * NQ) == h, SLOPES[h], slope_a)
    t_a = (past_len + ra % NQ).astype(F32)

    @pl.when(step == 0)
    def _():
        q = q_ref[...]
        qa = jnp.concatenate([q[:, 0:GROUP_W]] * (2 * N_HEADS), axis=0)
        r = _iota((rows_a, GROUP_W), 0)
        cidx = _iota((rows_a, GROUP_W), 1)
        keep = (cidx // HEAD_V == r // (2 * NQ)) & ((cidx % HEAD_V) // DIFF_DQK == (r // NQ) % 2)
        qa_s[...] = jnp.where(keep, qa, 0.0).astype(BF16)
        qb = jnp.concatenate([q[:, 768:1024]] * N_HEADS, axis=0)
        r = _iota((rows_b, GROUP_W), 0)
        cidx = _iota((rows_b, GROUP_W), 1)
        qb_s[...] = jnp.where(cidx // HEAD_V == r // NQ, qb, 0.0).astype(BF16)
        m_s[...] = jnp.full(m_s.shape, NEG, F32)
        l_s[...] = jnp.zeros(l_s.shape, F32)
        acc_s[...] = jnp.zeros(acc_s.shape, F32)
        carry_s[...] = jnp.zeros(carry_s.shape, F32)
        cum = _dot_rhs3(_tri_incl(NQ), lfc_ref[0])
        nq_s[...] = jnp.concatenate(
            [jnp.broadcast_to(cum[:, h:h + 1], (NQ, LANE)) for h in range(N_HEADS)], axis=0)

    def update(s, va_t, vb_t, nt):
        m_prev = m_s[...]
        m_new = jnp.maximum(m_prev, jnp.max(s, axis=1, keepdims=True))
        alpha = jnp.exp(m_prev - m_new)
        p = jnp.exp(s - m_new)
        l_s[...] = alpha * l_s[...] + jnp.sum(p, axis=1, keepdims=True)
        pb = p.astype(BF16)
        if nt:
            pv_a = _dot_nt(pb[0:rows_a], va_t)
            pv_b = _dot_nt(pb[rows_a:], vb_t)
        else:
            pv_a = _dot(pb[0:rows_a], va_t)
            pv_b = _dot(pb[rows_a:], vb_t)
        acc_s[...] = alpha * acc_s[...] + jnp.concatenate([pv_a, pv_b], axis=0)
        m_s[...] = m_new

    pages = lambda rs: jnp.concatenate([r[...].reshape(GROUP_W, PAGE).astype(BF16) for r in rs], axis=1)
    lane = _iota((1, width), 1)
    pos = ((n_pages - 1 - step * g_pages - lane // PAGE) * PAGE + lane % PAGE).astype(F32)
    strict = jnp.where(_iota((LANE, LANE), 0) > _iota((LANE, LANE), 1), 1.0, 0.0).astype(BF16)
    zrow = jnp.zeros((8 - N_HEADS, PAGE), F32)
    lf = jnp.concatenate([x for r in range(g_pages) for x in (lf_refs[r][...], zrow)], axis=0)
    inside = _dot_lhs3(lf, strict)
    tot = jnp.sum(lf, axis=1, keepdims=True)
    carry = carry_s[...]
    parts = []
    for r in range(g_pages):
        parts.append(inside[8 * r:8 * r + 8, :] + carry)
        carry = carry + tot[8 * r:8 * r + 8, :]
    carry_s[...] = carry
    csuf = jnp.concatenate(parts, axis=1)
    nq = jnp.concatenate([nq_s[...]] * g_pages, axis=1)
    s_a = _dot(qa_s[...], pages(ka_refs)) - slope_a * (t_a - pos)
    s_b = _dot(qb_s[...], pages(kb_refs)) + nq + _rep_rows(csuf)
    update(jnp.concatenate([s_a, s_b], axis=0), pages(va_refs), pages(vb_refs), True)

    @pl.when(step == n_steps - 1)
    def _():
        kvn = kvn_ref[...]
        zpad = jnp.zeros((PAGE - NQ, GROUP_W), F32)
        kn_a = jnp.concatenate([kvn[:, 0:256], zpad], axis=0).astype(BF16)
        vn_a = jnp.concatenate([kvn[:, 256:512], zpad], axis=0).astype(BF16)
        kn_b = jnp.concatenate([kvn[:, 512:768], zpad], axis=0).astype(BF16)
        vn_b = jnp.concatenate([kvn[:, 768:1024], zpad], axis=0).astype(BF16)
        jl = _iota((1, LANE), 1)
        ok_a = jl <= ra % NQ
        rb = _iota((rows_b, 1), 0)
        ok_b = jl <= rb % NQ
        cum_row = _dot_lhs3(lfr_ref[0], jnp.where(_iota((LANE, LANE), 0) <= _iota((LANE, LANE), 1),
                                                  1.0, 0.0).astype(BF16))
        sn_a = _dot_nt(qa_s[...], kn_a) - slope_a * ((ra % NQ).astype(F32) - jl.astype(F32))
        sn_b = _dot_nt(qb_s[...], kn_b) + nq_s[...] - _rep_rows(cum_row)
        update(jnp.concatenate([jnp.where(ok_a, sn_a, NEG), jnp.where(ok_b, sn_b, NEG)], axis=0),
               vn_a, vn_b, False)

        lam = _lam(lp_ref, lam_init)
        acc = acc_s[...] / l_s[...]
        col_h = _iota((NQ, GROUP_W), 1) // HEAD_V
        o_a = jnp.zeros((NQ, GROUP_W), F32)
        o_b = jnp.zeros((NQ, GROUP_W), F32)
        for h in range(N_HEADS):
            r0 = 2 * NQ * h
            o_a = o_a + jnp.where(col_h == h, acc[r0:r0 + NQ] - lam * acc[r0 + NQ:r0 + 2 * NQ], 0.0)
            rb0 = rows_a + NQ * h
            o_b = o_b + jnp.where(col_h == h, acc[rb0:rb0 + NQ], 0.0)
        mavg = _seg_mean_matrix(GROUP_W, HEAD_V)
        g = g_ref[...]
        o_ref[:, 0:GROUP_W] = _head_rms(o_a, mavg, g[:, 0:256] * (1.0 - lam_init))
        o_ref[:, GROUP_W:2 * GROUP_W] = _head_rms(o_b, mavg, g[:, 256:512])


def _attn_sample(page_table, caches, layer, qkv, kvn, lf_row, lf_col, lam_params_l, g_out_l, lam_init):
    ck_a, cv_a, ck_b, cv_b, c_lf = caches
    nb, n_pages = page_table.shape
    g_pages = PAGES_PER_STEP
    assert n_pages % g_pages == 0
    n_steps = n_pages // g_pages
    past_len = n_pages * PAGE

    tok = lambda n: pl.BlockSpec((NQ, n), lambda i, s, pt: (i, 0))
    small = lambda: pl.BlockSpec((1, 8, LANE), lambda i, s, pt: (i, 0, 0))
    in_specs = ([pl.BlockSpec(memory_space=pl.ANY)] * 5
                + [tok(W_AB), tok(1024), small(), small(),
                   pl.BlockSpec((4, DIFF_DQK), lambda i, s, pt: (0, 0)),
                   pl.BlockSpec((1, 1024), lambda i, s, pt: (0, 0))])
    rows = 3 * N_HEADS * NQ
    page_buf = lambda: pltpu.VMEM((2, g_pages, N_HEADS, HEAD_V, PAGE), F32)
    return pl.pallas_call(
        functools.partial(_dec_kernel, layer=layer, lam_init=lam_init, past_len=past_len, n_steps=n_steps),
        grid_spec=pltpu.PrefetchScalarGridSpec(
            num_scalar_prefetch=1,
            grid=(nb, n_steps),
            in_specs=in_specs,
            out_specs=pl.BlockSpec((NQ, 2 * GROUP_W), lambda i, s, pt: (i, 0)),
            scratch_shapes=[pltpu.VMEM((2 * N_HEADS * NQ, GROUP_W), BF16),
                            pltpu.VMEM((N_HEADS * NQ, GROUP_W), BF16),
                            pltpu.VMEM((rows, 1), F32), pltpu.VMEM((rows, 1), F32),
                            pltpu.VMEM((rows, GROUP_W), F32),
                            pltpu.VMEM((8, 1), F32), pltpu.VMEM((N_HEADS * NQ, LANE), F32),
                            page_buf(), page_buf(), page_buf(), page_buf(),
                            pltpu.VMEM((2, g_pages, N_HEADS, PAGE), F32),
                            pltpu.SemaphoreType.DMA((2, 5, g_pages))]),
        out_shape=jax.ShapeDtypeStruct((nb * NQ, 2 * GROUP_W), F32),
        compiler_params=_params(("arbitrary", "arbitrary")),
        name="attn_sample",
    )(page_table, ck_a, cv_a, ck_b, cv_b, c_lf, qkv, kvn, lf_row, lf_col, lam_params_l, g_out_l)


def kernel(x_prompt, x_sample, cache_a_k, cache_a_v, cache_b_k, cache_b_v, cache_b_logf, state_c, state_d,
           page_table, meta_tokens, w_in, b_f, w_c_gate, b_c_gate, lam_params, g_out, w_out, ln1_g, ln1_b,
           w_ffn_in, w_ffn_out, ln2_g, ln2_b):
    depth = w_in.shape[0]
    alpha = (2.0 * depth) ** 0.25
    nb, seq, _ = x_prompt.shape
    db, nq, _ = x_sample.shape
    assert nq == NQ
    l_true = N_META + seq
    lp = -(-l_true // BLK) * BLK
    xp = jnp.concatenate([jnp.broadcast_to(meta_tokens.astype(F32), (nb, N_META, D_MODEL)), x_prompt,
                          jnp.zeros((nb, lp - l_true, D_MODEL), F32)], axis=1).reshape(nb * lp, D_MODEL)
    xs = x_sample.reshape(db * nq, D_MODEL)
    tm_p = 512 if (nb * lp) % 512 == 0 else BLK
    tm_s = db * nq

    tr = lambda c: jnp.transpose(c, (0, 1, 3, 4, 2))
    caches = (tr(cache_a_k), tr(cache_a_v), tr(cache_b_k), tr(cache_b_v), jnp.transpose(cache_b_logf, (0, 1, 3, 2)))

    w_all = _prep_w_in(jnp.transpose(w_in, (0, 2, 1)))
    kvs, lgs, p_states, s_rows = [], [], [], []
    for l in range(depth):
        lam_init = 0.8 - 0.6 * math.exp(-0.3 * l)
        w = w_all[l]
        bf = jnp.pad(b_f[l], (0, LANE - N_HEADS)).reshape(1, LANE)
        wcg = jnp.pad(w_c_gate[l], ((0, LANE - GLA_RANK), (0, 0)))
        bcg = b_c_gate[l].reshape(1, LANE)
        gl = g_out[l].reshape(1, 1024)
        wo = w_out[l].astype(BF16)
        wfi = w_ffn_in[l].astype(BF16)
        wfo = w_ffn_out[l].astype(BF16)
        vec = lambda a: a[l].reshape(1, D_MODEL)
        post = functools.partial(_post, wo=wo, g1=vec(ln1_g), b1=vec(ln1_b), wfi=wfi, wfo=wfo,
                                 g2=vec(ln2_g), b2=vec(ln2_b), alpha=alpha)

        qkv, kv, cd, lg, vt = _proj(xp, w, bf, wcg, bcg, tm_p, BF16, LOG2E)
        lg3 = lg.reshape(nb, lp, 2 * LANE)
        kvs.append(kv.reshape(nb, lp, 1024))
        lgs.append(lg3)
        cab = _attn_prompt(qkv.reshape(nb, lp, W_AB), vt, lg3, lam_params[l], gl, lam_init)
        zero_state = jnp.zeros((nb, LANE, HEAD_V), F32)
        ccd, sc_p, sd_p = _lin(cd.reshape(nb, lp, W_CD), lg3, zero_state, zero_state, gl, l_true, BF16)
        xp = post(xp, cab.reshape(nb * lp, 512), ccd.reshape(nb * lp, 512), tm=tm_p)
        p_states.append((sc_p.reshape(nb, N_HEADS, LIN_DK, HEAD_V), sd_p.reshape(nb, N_HEADS, LIN_DK, HEAD_V)))

        qkv_s, kv_s, cd_s, lg_s, _ = _proj(xs, w, bf, wcg, bcg, tm_s, F32, 1.0)
        lf_s = lg_s[:, 0:N_HEADS].reshape(db, nq, N_HEADS)
        lf_col = jnp.pad(lf_s, ((0, 0), (0, 0), (0, LANE - N_HEADS)))
        lf_row = jnp.pad(jnp.transpose(lf_s, (0, 2, 1)), ((0, 0), (0, 8 - N_HEADS), (0, LANE - nq)))
        cab_s = _attn_sample(page_table, caches, l, qkv_s, kv_s, lf_row, lf_col, lam_params[l], gl, lam_init)
        pad_rows = lambda a: jnp.pad(a.reshape(db, nq, a.shape[-1]), ((0, 0), (0, BLK - nq), (0, 0)))
        ccd_s, sc_s, sd_s = _lin(pad_rows(cd_s), pad_rows(lg_s), state_c[l].reshape(db, LANE, HEAD_V),
                                 state_d[l].reshape(db, LANE, HEAD_V), gl, nq, F32)
        xs = post(xs, cab_s, ccd_s[:, 0:nq].reshape(db * nq, 512), tm=tm_s)
        heads_s = lambda a: a.reshape(db, nq, N_HEADS, HEAD_V)
        s_rows.append((heads_s(kv_s[:, 0:256]), heads_s(kv_s[:, 256:512]), heads_s(kv_s[:, 512:768]),
                       heads_s(kv_s[:, 768:1024]), lf_s,
                       sc_s.reshape(db, N_HEADS, LIN_DK, HEAD_V), sd_s.reshape(db, N_HEADS, LIN_DK, HEAD_V)))

    y_prompt = xp.reshape(nb, lp, D_MODEL)[:, N_META:l_true]
    y_sample = xs.reshape(db, nq, D_MODEL)
    ka_t, va_t, kb_t, vb_t, lf_t = _rows(kvs, lgs, l_true)
    heads_t = lambda a: jnp.transpose(a.reshape(depth, nb, N_HEADS, HEAD_V, l_true), (0, 1, 4, 2, 3))
    p_out = [heads_t(ka_t), heads_t(va_t), heads_t(kb_t), heads_t(vb_t), jnp.transpose(lf_t, (0, 1, 3, 2))]
    p_out += [jnp.stack(z) for z in zip(*p_states)]
    s_out = [jnp.stack(z) for z in zip(*s_rows)]
    return (y_prompt, y_sample, *p_out, *s_out)
```

```python
import functools
import math

import jax
import jax.numpy as jnp
from jax import lax
from jax.experimental import pallas as pl
from jax.experimental.pallas import tpu as pltpu

F32 = jnp.float32
BF16 = jnp.bfloat16

D_MODEL = 1024
N_META = 16
N_HEADS = 4
HEAD_V = 64
GROUP_W = 256
DIFF_DQK = 32
LIN_DK = 32
GLA_RANK = 16
GLA_TAU = 16.0
D_FF = 2816
EPS = 1e-5
NEG = -1e30
PAGE = 128
BLK = 128
LANE = 128
A_SCALE = DIFF_DQK ** -0.5
B_SCALE = HEAD_V ** -0.5
LIN_SCALE = LIN_DK ** -0.5
SLOPES = tuple(2.0 ** (-8.0 * (h + 1.0) / N_HEADS) for h in range(N_HEADS))
LOG_GAMMA = tuple(math.log1p(-(2.0 ** (-5.0 - h))) for h in range(N_HEADS))
LOG2E = math.log2(math.e)
IN_SIZES = (256, 256, 256, 256, 256, 256, 4, 128, 128, 256, 256, 16, 128, 128, 256, 256)
VMEM_LIMIT = 56 * 1024 * 1024


def _dot(a, b):
    return jnp.dot(a, b, preferred_element_type=F32)


def _dot_nt(a, b):
    return lax.dot_general(a, b, (((1,), (1,)), ((), ())), preferred_element_type=F32)


def _split3(x):
    hi = x.astype(BF16)
    r = x - hi.astype(F32)
    mid = r.astype(BF16)
    lo = (r - mid.astype(F32)).astype(BF16)
    return hi, mid, lo


def _dot_lhs3(x, m):
    hi, mid, lo = _split3(x)
    return _dot(hi, m) + _dot(mid, m) + _dot(lo, m)


def _dot_rhs3(m, x):
    hi, mid, lo = _split3(x)
    return _dot(m, hi) + _dot(m, mid) + _dot(m, lo)


def _iota(shape, dim):
    return lax.broadcasted_iota(jnp.int32, shape, dim)


def _tri_incl(n):
    return jnp.where(_iota((n, n), 0) >= _iota((n, n), 1), 1.0, 0.0).astype(BF16)


def _seg_mean_matrix(width, seg):
    same = (_iota((width, width), 0) // seg) == (_iota((width, width), 1) // seg)
    return jnp.where(same, 1.0 / seg, 0.0).astype(BF16)


def _block_diag_mask():
    r = _iota((LANE, GROUP_W), 0) // LIN_DK
    c = _iota((LANE, GROUP_W), 1) // HEAD_V
    return jnp.where(r == c, 1.0, 0.0).astype(F32)


def _log_sigmoid(x):
    return jnp.minimum(x, 0.0) - jnp.log1p(jnp.exp(-jnp.abs(x)))


def _silu(x):
    return x / (1.0 + jnp.exp(-x))


def _head_rms(o, mavg, gain):
    ms = _dot_lhs3(o * o, mavg)
    return o * lax.rsqrt(ms + EPS) * gain


def _layer_norm(x, g, b):
    mu = jnp.mean(x, axis=-1, keepdims=True)
    xc = x - mu
    var = jnp.mean(xc * xc, axis=-1, keepdims=True)
    return xc * lax.rsqrt(var + EPS) * g + b


def _lam(lp_ref, lam_init):
    lp = lp_ref[...]
    a = jnp.sum(lp[0:1, :] * lp[1:2, :], axis=1, keepdims=True)
    b = jnp.sum(lp[2:3, :] * lp[3:4, :], axis=1, keepdims=True)
    return jnp.exp(a) - jnp.exp(b) + lam_init


def _params(sem):
    return pltpu.CompilerParams(dimension_semantics=sem, vmem_limit_bytes=VMEM_LIMIT)


def _const_spec(shape):
    n = len(shape)
    return pl.BlockSpec(shape, lambda *_: (0,) * n, pipeline_mode=pl.Buffered(1))


W_AB = 1536
W_CD = 1536
W_FG = 256
W_ALL = W_AB + W_CD + W_FG


def _prep_w_in(w_in_t):
    offs = [0]
    for s in IN_SIZES:
        offs.append(offs[-1] + s)
    seg = lambda i: w_in_t[:, offs[i]:offs[i + 1], :]
    pad = lambda a: jnp.pad(a, ((0, 0), (0, LANE - a.shape[1]), (0, 0)))
    rows = [seg(i) for i in (0, 1, 2, 3, 4, 5)] + [seg(i) for i in (7, 8, 9, 10, 12, 13, 14, 15)]
    rows += [pad(seg(6)), pad(seg(11))]
    return jnp.concatenate(rows, axis=1).astype(BF16)


def _proj_kernel(x_ref, w_ref, bf_ref, wcg_ref, bcg_ref, qkv_ref, kv_ref, cd_ref, lg_ref, vt_ref, *, q_scale):
    xb = x_ref[...].astype(BF16)
    ab = _dot_nt(xb, w_ref[0:W_AB, :])
    dt = qkv_ref.dtype
    qkv_ref[:, 0:256] = (ab[:, 0:256] * (A_SCALE * q_scale)).astype(dt)
    qkv_ref[:, 256:768] = ab[:, 256:768].astype(dt)
    qkv_ref[:, 768:1024] = (ab[:, 768:1024] * (B_SCALE * q_scale)).astype(dt)
    qkv_ref[:, 1024:1536] = ab[:, 1024:1536].astype(dt)
    kv_ref[:, 0:512] = ab[:, 256:768]
    kv_ref[:, 512:1024] = ab[:, 1024:1536]
    vt_ref[0:GROUP_W, :] = ab[:, 512:768].T.astype(BF16)
    vt_ref[GROUP_W:2 * GROUP_W, :] = ab[:, 1280:1536].T.astype(BF16)
    cd_ref[...] = _dot_nt(xb, w_ref[W_AB:W_AB + W_CD, :])
    fa = _dot_nt(xb, w_ref[W_AB + W_CD:W_ALL, :])
    lg_ref[:, 0:LANE] = _log_sigmoid(fa[:, 0:LANE] + bf_ref[...])
    a = fa[:, LANE:2 * LANE]
    a_hi = a.astype(BF16)
    a_lo = (a - a_hi.astype(F32)).astype(BF16)
    w = wcg_ref[...]
    w_hi = w.astype(BF16)
    w_lo = (w - w_hi.astype(F32)).astype(BF16)
    z = _dot(a_hi, w_hi) + _dot(a_hi, w_lo) + _dot(a_lo, w_hi) + bcg_ref[...]
    lg_ref[:, LANE:2 * LANE] = _log_sigmoid(z) / GLA_TAU


def _proj(x, w, bf, wcg, bcg, tm, qkv_dtype, q_scale):
    t = x.shape[0]
    row = lambda n: pl.BlockSpec((tm, n), lambda i: (i, 0))
    return pl.pallas_call(
        functools.partial(_proj_kernel, q_scale=q_scale),
        grid=(t // tm,),
        in_specs=[row(D_MODEL), _const_spec((W_ALL, D_MODEL)), _const_spec((1, LANE)),
                  _const_spec((LANE, LANE)), _const_spec((1, LANE))],
        out_specs=[row(W_AB), row(1024), row(W_CD), row(2 * LANE),
                   pl.BlockSpec((2 * GROUP_W, tm), lambda i: (0, i))],
        out_shape=[jax.ShapeDtypeStruct((t, W_AB), qkv_dtype),
                   jax.ShapeDtypeStruct((t, 1024), F32),
                   jax.ShapeDtypeStruct((t, W_CD), F32),
                   jax.ShapeDtypeStruct((t, 2 * LANE), F32),
                   jax.ShapeDtypeStruct((2 * GROUP_W, t), BF16)],
        compiler_params=_params(("parallel",)),
        name="proj",
    )(x, w, bf, wcg, bcg)


def _emit_rows(kv_refs, lg_refs, out_refs):
    ka_ref, va_ref, kb_ref, vb_ref, lf_ref = out_refs
    for l in range(len(kv_refs)):
        kv = kv_refs[l][0]
        ka_ref[l, 0] = kv[:, 0:256].T
        va_ref[l, 0] = kv[:, 256:512].T
        kb_ref[l, 0] = kv[:, 512:768].T
        vb_ref[l, 0] = kv[:, 768:1024].T
        lf_ref[l, 0] = lg_refs[l][0].T[0:N_HEADS, :]


BIAS_F0 = 3 * N_HEADS


A_W = 4 * BLK
B_W = 2 * BLK
S_W = 2 * A_W + 2 * B_W


def _map_lanes(h, m):
    j, half = divmod(h, 2)
    if m < 2:
        return j * A_W + (2 * half + m) * BLK
    return 2 * A_W + j * B_W + half * BLK


def _attn_kernel(*refs, lam_init, nblk, n_rows):
    qa_ref, ka_ref, qb_ref, kb_ref, vt_ref, lf_ref, lp_ref, g_ref = refs[0:8]
    n_in = 8 + 2 * n_rows
    o_ref = refs[n_in]
    kbias, wqa, wqb, m_ref, l_ref, acc_ref, s_even, s_odd = refs[-8:]
    if n_rows:
        _emit_rows(refs[8:8 + n_rows], refs[8 + n_rows:n_in], refs[n_in + 1:n_in + 6])
    qi = pl.program_id(1)
    row = _iota((BLK, BLK), 0)
    col = _iota((BLK, BLK), 1)

    @pl.when(qi == 0)
    def _():
        tri = _tri_incl(BLK)
        sel = [[jnp.where((row < N_HEADS) & (col == base + 3 * row + i), 1.0, 0.0).astype(BF16)
                for i in range(3)] for base in (0, BIAS_F0)]
        slope = _lane_const(tuple(LOG2E * s for s in SLOPES) + (0.0,), LANE, 1)
        carry = jnp.zeros((1, LANE), F32)
        for i in range(nblk):
            cs = _dot_rhs3(tri, lf_ref[0, i * BLK:(i + 1) * BLK, :]) + carry
            carry = cs[BLK - 1:BLK, :]
            pos = (i * BLK + row).astype(F32)
            acc = jnp.zeros((BLK, LANE), F32)
            for terms, mats in ((_split3(slope * pos), sel[0]), (_split3(cs * -LOG2E), sel[1])):
                for t, mat in zip(terms, mats):
                    acc = acc + _dot(t, mat)
            kbias[i * BLK:(i + 1) * BLK, :] = acc.astype(BF16)

    m_ref[...] = jnp.full(m_ref.shape, NEG, F32)
    l_ref[...] = jnp.zeros(l_ref.shape, F32)
    acc_ref[...] = jnp.zeros(acc_ref.shape, F32)

    qa = qa_ref[0].astype(F32)
    qb = qb_ref[0].astype(F32)
    for j in range(2):
        qa_t = qa[:, j * LANE:(j + 1) * LANE].T
        qb_t = qb[:, j * LANE:(j + 1) * LANE].T
        for half in range(2):
            h = 2 * j + half
            lo = half * HEAD_V
            w_alibi = jnp.where((row >= 3 * h) & (row < 3 * h + 3), 1.0, 0.0).astype(BF16)
            w_fox = jnp.where((row >= BIAS_F0 + 3 * h) & (row < BIAS_F0 + 3 * h + 3), 1.0, 0.0).astype(BF16)
            for m in range(2):
                keep = (row >= lo + m * DIFF_DQK) & (row < lo + (m + 1) * DIFF_DQK)
                c0 = (2 * half + m) * BLK
                wqa[j, 0:LANE, c0:c0 + BLK] = jnp.where(keep, qa_t, 0.0).astype(BF16)
                wqa[j, LANE:2 * LANE, c0:c0 + BLK] = w_alibi
            keep = (row >= lo) & (row < lo + HEAD_V)
            wqb[j, 0:LANE, half * BLK:(half + 1) * BLK] = jnp.where(keep, qb_t, 0.0).astype(BF16)
            wqb[j, LANE:2 * LANE, half * BLK:(half + 1) * BLK] = w_fox

    def scores(kj):
        ks = pl.multiple_of(kj * BLK, BLK)
        bias = kbias[pl.ds(ks, BLK), :]
        parts = []
        for k_ref, w_ref in ((ka_ref, wqa), (kb_ref, wqb)):
            for j in range(2):
                kk = jnp.concatenate([k_ref[0, pl.ds(ks, BLK), j * LANE:(j + 1) * LANE], bias], axis=1)
                parts.append(_dot(kk, w_ref[j]))
        return jnp.concatenate(parts, axis=1)

    def update(kj, s):
        ks = pl.multiple_of(kj * BLK, BLK)
        m_prev = m_ref[...]
        m_new = jnp.maximum(m_prev, jnp.max(s, axis=0, keepdims=True))
        alpha = jnp.exp2(m_prev - m_new)
        p = jnp.exp2(s - m_new)
        l_ref[...] = alpha * l_ref[...] + jnp.sum(p, axis=0, keepdims=True)
        m_ref[...] = m_new
        pb = p.astype(BF16)
        for h in range(N_HEADS):
            vta = vt_ref[h * HEAD_V:(h + 1) * HEAD_V, pl.ds(ks, BLK)]
            vtb = vt_ref[GROUP_W + h * HEAD_V:GROUP_W + (h + 1) * HEAD_V, pl.ds(ks, BLK)]
            for c0, w, vt in ((_map_lanes(h, 0), 2 * BLK, vta), (_map_lanes(h, 2), BLK, vtb)):
                acc_ref[:, c0:c0 + w] = alpha[:, c0:c0 + w] * acc_ref[:, c0:c0 + w] + _dot(vt, pb[:, c0:c0 + w])

    def diag(s):
        return jnp.where(_iota((BLK, S_W), 0) <= _iota((BLK, S_W), 1) % BLK, s, NEG)

    s_even[...] = scores(0)

    def pair(i, carry):
        s_odd[...] = scores(2 * i + 1)
        update(2 * i, s_even[...])
        s_even[...] = scores(2 * i + 2)
        update(2 * i + 1, s_odd[...])
        return carry

    lax.fori_loop(0, qi // 2, pair, 0)

    @pl.when(qi % 2 == 0)
    def _():
        update(qi, diag(s_even[...]))

    @pl.when(qi % 2 == 1)
    def _():
        s_odd[...] = scores(qi)
        update(qi - 1, s_even[...])
        update(qi, diag(s_odd[...]))

    lam = _lam(lp_ref, lam_init)
    g = g_ref[...]
    inv_l = 1.0 / l_ref[...]

    def head_out(h, m):
        c0 = _map_lanes(h, m)
        return acc_ref[:, c0:c0 + BLK] * inv_l[:, c0:c0 + BLK]

    def rms_t(o):
        return o * lax.rsqrt(jnp.mean(o * o, axis=0, keepdims=True) + EPS)

    for j in range(2):
        oa, ob = [], []
        for half in range(2):
            h = 2 * j + half
            oa.append(rms_t(head_out(h, 0) - lam * head_out(h, 1)))
            ob.append(rms_t(head_out(h, 2)))
        ga = g[:, j * LANE:(j + 1) * LANE] * (1.0 - lam_init)
        gb = g[:, GROUP_W + j * LANE:GROUP_W + (j + 1) * LANE]
        o_ref[0, :, j * LANE:(j + 1) * LANE] = (jnp.concatenate(oa, axis=0).T * ga).astype(o_ref.dtype)
        o_ref[0, :, GROUP_W + j * LANE:GROUP_W + (j + 1) * LANE] = (
            jnp.concatenate(ob, axis=0).T * gb).astype(o_ref.dtype)


def _attn_prompt(qkv, vt, lg, lam_params_l, g_out_l, lam_init, row_src=None, l_true=None):
    b, lp, _ = qkv.shape
    nblk = lp // BLK
    assert nblk <= 256
    qspec = lambda c: pl.BlockSpec((1, BLK, GROUP_W), lambda i, j: (i, j, c))
    kspec = lambda c: pl.BlockSpec((1, lp, GROUP_W), lambda i, j: (i, 0, c))
    kvs, lgs = row_src if row_src else ((), ())
    depth = len(kvs)
    row_in = ([pl.BlockSpec((1, BLK, 1024), lambda i, j: (i, j, 0))] * depth
              + [pl.BlockSpec((1, BLK, LANE), lambda i, j: (i, j, 0))] * depth)
    row_out, row_shape = [], []
    if depth:
        row_out = [pl.BlockSpec((depth, 1, GROUP_W, BLK), lambda i, j: (0, i, 0, j))] * 4
        row_out += [pl.BlockSpec((depth, 1, N_HEADS, BLK), lambda i, j: (0, i, 0, j))]
        row_shape = [jax.ShapeDtypeStruct((depth, b, GROUP_W, l_true), F32)] * 4
        row_shape += [jax.ShapeDtypeStruct((depth, b, N_HEADS, l_true), F32)]
    return pl.pallas_call(
        functools.partial(_attn_kernel, lam_init=lam_init, nblk=nblk, n_rows=depth),
        grid=(b, nblk),
        in_specs=[qspec(0), kspec(1), qspec(3), kspec(4),
                  pl.BlockSpec((2 * GROUP_W, lp), lambda i, j: (0, i)),
                  pl.BlockSpec((1, lp, LANE), lambda i, j: (i, 0, 0)),
                  _const_spec((4, DIFF_DQK)), _const_spec((1, 1024))] + row_in,
        out_specs=[pl.BlockSpec((1, BLK, 2 * GROUP_W), lambda i, j: (i, j, 0))] + row_out,
        out_shape=[jax.ShapeDtypeStruct((b, lp, 2 * GROUP_W), BF16)] + row_shape,
        scratch_shapes=[pltpu.VMEM((lp, LANE), BF16),
                        pltpu.VMEM((2, 2 * LANE, A_W), BF16), pltpu.VMEM((2, 2 * LANE, B_W), BF16),
                        pltpu.VMEM((1, S_W), F32), pltpu.VMEM((1, S_W), F32), pltpu.VMEM((HEAD_V, S_W), F32),
                        pltpu.VMEM((BLK, S_W), F32), pltpu.VMEM((BLK, S_W), F32)],
        compiler_params=_params(("parallel", "arbitrary")),
        name="attn_prompt",
    )(qkv, qkv, qkv, qkv, vt, lg, lam_params_l, g_out_l, *kvs, *lgs)


SUB = 32


def _lane_const(values, width, seg):
    lane = _iota((1, width), 1) // seg
    out = jnp.full((1, width), values[-1], F32)
    for h in range(len(values) - 2, -1, -1):
        out = jnp.where(lane == h, values[h], out)
    return out


def _state_rows(s_bd):
    hsum = s_bd[:, 0:LANE] + s_bd[:, LANE:2 * LANE]
    return (hsum + pltpu.roll(hsum, HEAD_V, 1))[:, 0:HEAD_V]


def _lin_kernel(cd_ref, g_ref, sc0_ref, sd0_ref, gout_ref, cat_ref, sc_ref, sd_ref, s_c, s_d, o_s,
                *, valid_len, n_tiles, nchunks):
    c = pl.program_id(1)
    bd = _block_diag_mask()
    e2 = bd.astype(BF16)

    @pl.when(c == 0)
    def _():
        expand = jnp.where(_iota((HEAD_V, GROUP_W), 0) == _iota((HEAD_V, GROUP_W), 1) % HEAD_V,
                           1.0, 0.0).astype(BF16)
        s_c[...] = _dot_lhs3(sc0_ref[0], expand) * bd
        s_d[...] = _dot_lhs3(sd0_ref[0], expand) * bd

    rowi = _iota((BLK, 1), 0)
    valid = (c * BLK + rowi) < valid_len
    n_v = jnp.clip(valid_len - c * BLK, 0, BLK).astype(F32)
    x = cd_ref[0]
    g = jnp.where(valid, g_ref[0], 0.0)
    q_c = x[:, 0:128] * LIN_SCALE
    k_c = jnp.where(valid, x[:, 128:256], 0.0)
    v_c = x[:, 256:512]
    r_c = x[:, 512:768]
    q_d = x[:, 768:896]
    k_d = jnp.where(valid, x[:, 896:1024], 0.0) * LIN_SCALE
    v_d = x[:, 1024:1280]
    r_d = x[:, 1280:1536]

    b = _dot_rhs3(_tri_incl(BLK), g)
    b_last = b[BLK - 1:BLK, :]
    s_prev = s_c[...]
    o_s[...] = _dot((q_c * jnp.exp(b)).astype(BF16), s_prev.astype(BF16))
    lane_h = _iota((SUB, LANE), 1) // LIN_DK
    col_h = _iota((SUB, GROUP_W), 1) // HEAD_V
    vcb = v_c.astype(BF16)
    for sb in range(-(-n_tiles * 8 // SUB)):
        r0 = sb * SUB
        for ti in range(min(SUB // 8, n_tiles - r0 // 8)):
            t0 = r0 + 8 * ti
            ns = 8 * (ti + 1)
            ws = []
            for r in range(8):
                t = t0 + r
                diff = b[t:t + 1, :] - b[r0:r0 + ns, :]
                dec = jnp.exp(jnp.where(rowi[r0:r0 + ns] <= t, diff, NEG))
                ws.append(dec * (k_c[r0:r0 + ns] * q_c[t:t + 1, :]))
            att = _dot(jnp.concatenate(ws, axis=0).astype(BF16), e2)
            o_t = jnp.sum(att.reshape(8, ns, GROUP_W) * v_c[r0:r0 + ns][None], axis=1)
            o_s[t0:t0 + 8, :] = o_s[t0:t0 + 8, :] + o_t
        if sb > 0:
            ref = b[r0 - 1:r0, :]
            qt = q_c[r0:r0 + SUB] * jnp.exp(b[r0:r0 + SUB] - ref)
            ke = (k_c[0:r0] * jnp.exp(ref - b[0:r0])).astype(BF16)
            qs = jnp.concatenate([jnp.where(lane_h == h, qt, 0.0) for h in range(N_HEADS)], axis=0)
            full = _dot(_dot_nt(qs.astype(BF16), ke).astype(BF16), vcb[0:r0])
            o_off = jnp.zeros((SUB, GROUP_W), F32)
            for h in range(N_HEADS):
                o_off = o_off + jnp.where(col_h == h, full[h * SUB:(h + 1) * SUB], 0.0)
            o_s[r0:r0 + SUB, :] = o_s[r0:r0 + SUB, :] + o_off
    o_c = o_s[...]
    kt = k_c * jnp.exp(b_last - b)
    d_col = jnp.broadcast_to(jnp.exp(b_last), (LANE, LANE)).T
    s_new = s_prev * jnp.concatenate([d_col, d_col], axis=1) + _dot(kt.T.astype(BF16), v_c.astype(BF16)) * bd
    s_c[...] = s_new

    lgl = _lane_const(LOG_GAMMA, LANE, LIN_DK)
    rowf = rowi.astype(F32)
    d_prev = s_d[...]
    o_d = _dot((q_d * jnp.exp((rowf + 1.0) * lgl)).astype(BF16), d_prev.astype(BF16))
    lane_h = _iota((BLK, LANE), 1) // LIN_DK
    col_h = _iota((BLK, GROUP_W), 1) // HEAD_V
    rr = _iota((BLK, BLK), 0)
    cc = _iota((BLK, BLK), 1)
    dist = (rr - cc).astype(F32)
    kdb = k_d.astype(BF16)
    vdb = v_d.astype(BF16)
    for h in range(N_HEADS):
        qm = jnp.where(lane_h == h, q_d, 0.0).astype(BF16)
        dec = jnp.exp(jnp.where(rr >= cc, dist * LOG_GAMMA[h], NEG))
        att = (_dot_nt(qm, kdb) * dec).astype(BF16)
        o_d = o_d + jnp.where(col_h == h, _dot(att, vdb), 0.0)
    deck = jnp.where(valid, jnp.exp((n_v - 1.0 - rowf) * lgl), 0.0)
    lgr = jnp.full((LANE, 1), LOG_GAMMA[-1], F32)
    rrow = _iota((LANE, 1), 0) // LIN_DK
    for h in range(N_HEADS - 2, -1, -1):
        lgr = jnp.where(rrow == h, LOG_GAMMA[h], lgr)
    d_new = d_prev * jnp.exp(n_v * lgr) + _dot((k_d * deck).T.astype(BF16), vdb) * bd
    s_d[...] = d_new

    mavg = _seg_mean_matrix(GROUP_W, HEAD_V)
    gout = gout_ref[...]
    cat_ref[0, :, 0:GROUP_W] = (_silu(r_c) * _head_rms(o_c, mavg, gout[:, 512:768])).astype(cat_ref.dtype)
    cat_ref[0, :, GROUP_W:2 * GROUP_W] = (_silu(r_d) * _head_rms(o_d, mavg, gout[:, 768:1024])).astype(cat_ref.dtype)

    @pl.when(c == nchunks - 1)
    def _():
        sc_ref[0] = _state_rows(s_new)
        sd_ref[0] = _state_rows(d_new)


def _lin(cd, lg, sc0, sd0, g_out_l, valid_len, cat_dtype):
    b, lp, _ = cd.shape
    nchunks = lp // BLK
    n_tiles = BLK // 8 if nchunks > 1 else -(-valid_len // 8)
    st = pl.BlockSpec((1, LANE, HEAD_V), lambda i, j: (i, 0, 0))
    return pl.pallas_call(
        functools.partial(_lin_kernel, valid_len=valid_len, n_tiles=n_tiles, nchunks=nchunks),
        grid=(b, nchunks),
        in_specs=[pl.BlockSpec((1, BLK, W_CD), lambda i, j: (i, j, 0)),
                  pl.BlockSpec((1, BLK, LANE), lambda i, j: (i, j, 1)),
                  st, st, _const_spec((1, 1024))],
        out_specs=[pl.BlockSpec((1, BLK, 2 * GROUP_W), lambda i, j: (i, j, 0)), st, st],
        out_shape=[jax.ShapeDtypeStruct((b, lp, 2 * GROUP_W), cat_dtype),
                   jax.ShapeDtypeStruct((b, LANE, HEAD_V), F32),
                   jax.ShapeDtypeStruct((b, LANE, HEAD_V), F32)],
        scratch_shapes=[pltpu.VMEM((LANE, GROUP_W), F32), pltpu.VMEM((LANE, GROUP_W), F32),
                        pltpu.VMEM((BLK, GROUP_W), F32)],
        compiler_params=_params(("parallel", "arbitrary")),
        name="lin",
    )(cd, lg, sc0, sd0, g_out_l)


FF_CHUNK = 256


def _post_kernel(x_ref, cab_ref, ccd_ref, wo_ref, g1_ref, b1_ref, wfi_ref, wfo_ref, g2_ref, b2_ref, o_ref,
                 act_ref, *, alpha):
    y = (_dot(cab_ref[...].astype(BF16), wo_ref[0:512, :])
         + _dot(ccd_ref[...].astype(BF16), wo_ref[512:1024, :]))
    x1 = _layer_norm(alpha * x_ref[...] + y, g1_ref[...], b1_ref[...])
    x1b = x1.astype(BF16)
    for j in range(D_FF // FF_CHUNK):
        gate = _dot(x1b, wfi_ref[:, j * FF_CHUNK:(j + 1) * FF_CHUNK])
        up = _dot(x1b, wfi_ref[:, D_FF + j * FF_CHUNK:D_FF + (j + 1) * FF_CHUNK])
        act_ref[:, j * FF_CHUNK:(j + 1) * FF_CHUNK] = (_silu(gate) * up).astype(BF16)
    y2 = _dot(act_ref[...], wfo_ref[...])
    o_ref[...] = _layer_norm(alpha * x1 + y2, g2_ref[...], b2_ref[...])


def _post(x, cab, ccd, wo, g1, b1, wfi, wfo, g2, b2, tm, alpha):
    t = x.shape[0]
    row = lambda n: pl.BlockSpec((tm, n), lambda i: (i, 0))
    vec = _const_spec((1, D_MODEL))
    return pl.pallas_call(
        functools.partial(_post_kernel, alpha=alpha),
        grid=(t // tm,),
        in_specs=[row(D_MODEL), row(512), row(512), _const_spec((D_MODEL, D_MODEL)), vec, vec,
                  _const_spec((D_MODEL, 2 * D_FF)), _const_spec((D_FF, D_MODEL)), vec, vec],
        out_specs=row(D_MODEL),
        out_shape=jax.ShapeDtypeStruct((t, D_MODEL), F32),
        scratch_shapes=[pltpu.VMEM((tm, D_FF), BF16)],
        compiler_params=_params(("parallel",)),
        name="post",
    )(x, cab, ccd, wo, g1, b1, wfi, wfo, g2, b2)


PAGES_PER_STEP = 16
NQ = 8


def _rep_rows(x4):
    n = x4.shape[1]
    return jnp.concatenate([jnp.broadcast_to(x4[h:h + 1, :], (NQ, n)) for h in range(N_HEADS)], axis=0)


def _dec_kernel(pt_ref, ka_hbm, va_hbm, kb_hbm, vb_hbm, lf_hbm, q_ref, kvn_ref, lfr_ref, lfc_ref, lp_ref, g_ref,
                o_ref, qa_s, qb_s, m_s, l_s, acc_s, carry_s, nq_s, ka_buf, va_buf, kb_buf, vb_buf, lf_buf, sem,
                *, layer, lam_init, past_len, n_steps):
    g_pages = PAGES_PER_STEP
    seq = pl.program_id(0)
    step = pl.program_id(1)
    n_pages = n_steps * g_pages
    width = g_pages * PAGE

    hbms = (ka_hbm, va_hbm, kb_hbm, vb_hbm, lf_hbm)
    bufs = (ka_buf, va_buf, kb_buf, vb_buf, lf_buf)
    t = seq * n_steps + step
    slot = t % 2

    def page_copies(b_idx, s_idx, sl, known_pages):
        out = []
        for r in range(g_pages):
            page = 0 if known_pages else pt_ref[b_idx, n_pages - 1 - (s_idx * g_pages + r)]
            for c in range(len(hbms)):
                out.append(pltpu.make_async_copy(hbms[c].at[layer, page], bufs[c].at[sl, r], sem.at[sl, c, r]))
        return out

    @pl.when(t == 0)
    def _():
        for cp in page_copies(0, 0, 0, False):
            cp.start()

    @pl.when(t + 1 < pl.num_programs(0) * n_steps)
    def _():
        last = step + 1 == n_steps
        for cp in page_copies(jnp.where(last, seq + 1, seq), jnp.where(last, 0, step + 1), 1 - slot, False):
            cp.start()

    for cp in page_copies(seq, step, slot, True):
        cp.wait()
    ka_refs, va_refs, kb_refs, vb_refs, lf_refs = [[b.at[slot, r] for r in range(g_pages)] for b in bufs]

    rows_a = 2 * N_HEADS * NQ
    rows_b = N_HEADS * NQ
    ra = _iota((rows_a, 1), 0)
    slope_a = jnp.full((rows_a, 1), SLOPES[-1], F32)
    for h in range(N_HEADS - 2, -1, -1):
        slope_a = jnp.where(ra // (2 * NQ) == h, SLOPES[h], slope_a)
    t_a = (past_len + ra % NQ).astype(F32)

    @pl.when(step == 0)
    def _():
        q = q_ref[...]
        qa = jnp.concatenate([q[:, 0:GROUP_W]] * (2 * N_HEADS), axis=0)
        r = _iota((rows_a, GROUP_W), 0)
        cidx = _iota((rows_a, GROUP_W), 1)
        keep = (cidx // HEAD_V == r // (2 * NQ)) & ((cidx % HEAD_V) // DIFF_DQK == (r // NQ) % 2)
        qa_s[...] = jnp.where(keep, qa, 0.0).astype(BF16)
        qb = jnp.concatenate([q[:, 768:1024]] * N_HEADS, axis=0)
        r = _iota((rows_b, GROUP_W), 0)
        cidx = _iota((rows_b, GROUP_W), 1)
        qb_s[...] = jnp.where(cidx // HEAD_V == r // NQ, qb, 0.0).astype(BF16)
        m_s[...] = jnp.full(m_s.shape, NEG, F32)
        l_s[...] = jnp.zeros(l_s.shape, F32)
        acc_s[...] = jnp.zeros(acc_s.shape, F32)
        carry_s[...] = jnp.zeros(carry_s.shape, F32)
        cum = _dot_rhs3(_tri_incl(NQ), lfc_ref[0])
        nq_s[...] = jnp.concatenate(
            [jnp.broadcast_to(cum[:, h:h + 1], (NQ, LANE)) for h in range(N_HEADS)], axis=0)

    def update(s, va_t, vb_t, nt):
        m_prev = m_s[...]
        m_new = jnp.maximum(m_prev, jnp.max(s, axis=1, keepdims=True))
        alpha = jnp.exp(m_prev - m_new)
        p = jnp.exp(s - m_new)
        l_s[...] = alpha * l_s[...] + jnp.sum(p, axis=1, keepdims=True)
        pb = p.astype(BF16)
        if nt:
            pv_a = _dot_nt(pb[0:rows_a], va_t)
            pv_b = _dot_nt(pb[rows_a:], vb_t)
        else:
            pv_a = _dot(pb[0:rows_a], va_t)
            pv_b = _dot(pb[rows_a:], vb_t)
        acc_s[...] = alpha * acc_s[...] + jnp.concatenate([pv_a, pv_b], axis=0)
        m_s[...] = m_new

    pages = lambda rs: jnp.concatenate([r[...].reshape(GROUP_W, PAGE).astype(BF16) for r in rs], axis=1)
    lane = _iota((1, width), 1)
    pos = ((n_pages - 1 - step * g_pages - lane // PAGE) * PAGE + lane % PAGE).astype(F32)
    strict = jnp.where(_iota((LANE, LANE), 0) > _iota((LANE, LANE), 1), 1.0, 0.0).astype(BF16)
    zrow = jnp.zeros((8 - N_HEADS, PAGE), F32)
    lf = jnp.concatenate([x for r in range(g_pages) for x in (lf_refs[r][...], zrow)], axis=0)
    inside = _dot_lhs3(lf, strict)
    tot = jnp.sum(lf, axis=1, keepdims=True)
    carry = carry_s[...]
    parts = []
    for r in range(g_pages):
        parts.append(inside[8 * r:8 * r + 8, :] + carry)
        carry = carry + tot[8 * r:8 * r + 8, :]
    carry_s[...] = carry
    csuf = jnp.concatenate(parts, axis=1)
    nq = jnp.concatenate([nq_s[...]] * g_pages, axis=1)
    s_a = _dot(qa_s[...], pages(ka_refs)) - slope_a * (t_a - pos)
    s_b = _dot(qb_s[...], pages(kb_refs)) + nq + _rep_rows(csuf)
    update(jnp.concatenate([s_a, s_b], axis=0), pages(va_refs), pages(vb_refs), True)

    @pl.when(step == n_steps - 1)
    def _():
        kvn = kvn_ref[...]
        zpad = jnp.zeros((PAGE - NQ, GROUP_W), F32)
        kn_a = jnp.concatenate([kvn[:, 0:256], zpad], axis=0).astype(BF16)
        vn_a = jnp.concatenate([kvn[:, 256:512], zpad], axis=0).astype(BF16)
        kn_b = jnp.concatenate([kvn[:, 512:768], zpad], axis=0).astype(BF16)
        vn_b = jnp.concatenate([kvn[:, 768:1024], zpad], axis=0).astype(BF16)
        jl = _iota((1, LANE), 1)
        ok_a = jl <= ra % NQ
        rb = _iota((rows_b, 1), 0)
        ok_b = jl <= rb % NQ
        cum_row = _dot_lhs3(lfr_ref[0], jnp.where(_iota((LANE, LANE), 0) <= _iota((LANE, LANE), 1),
                                                  1.0, 0.0).astype(BF16))
        sn_a = _dot_nt(qa_s[...], kn_a) - slope_a * ((ra % NQ).astype(F32) - jl.astype(F32))
        sn_b = _dot_nt(qb_s[...], kn_b) + nq_s[...] - _rep_rows(cum_row)
        update(jnp.concatenate([jnp.where(ok_a, sn_a, NEG), jnp.where(ok_b, sn_b, NEG)], axis=0),
               vn_a, vn_b, False)

        lam = _lam(lp_ref, lam_init)
        acc = acc_s[...] / l_s[...]
        col_h = _iota((NQ, GROUP_W), 1) // HEAD_V
        o_a = jnp.zeros((NQ, GROUP_W), F32)
        o_b = jnp.zeros((NQ, GROUP_W), F32)
        for h in range(N_HEADS):
            r0 = 2 * NQ * h
            o_a = o_a + jnp.where(col_h == h, acc[r0:r0 + NQ] - lam * acc[r0 + NQ:r0 + 2 * NQ], 0.0)
            rb0 = rows_a + NQ * h
            o_b = o_b + jnp.where(col_h == h, acc[rb0:rb0 + NQ], 0.0)
        mavg = _seg_mean_matrix(GROUP_W, HEAD_V)
        g = g_ref[...]
        o_ref[:, 0:GROUP_W] = _head_rms(o_a, mavg, g[:, 0:256] * (1.0 - lam_init))
        o_ref[:, GROUP_W:2 * GROUP_W] = _head_rms(o_b, mavg, g[:, 256:512])


def _attn_sample(page_table, caches, layer, qkv, kvn, lf_row, lf_col, lam_params_l, g_out_l, lam_init):
    ck_a, cv_a, ck_b, cv_b, c_lf = caches
    nb, n_pages = page_table.shape
    g_pages = PAGES_PER_STEP
    assert n_pages % g_pages == 0
    n_steps = n_pages // g_pages
    past_len = n_pages * PAGE

    tok = lambda n: pl.BlockSpec((NQ, n), lambda i, s, pt: (i, 0))
    small = lambda: pl.BlockSpec((1, 8, LANE), lambda i, s, pt: (i, 0, 0))
    in_specs = ([pl.BlockSpec(memory_space=pl.ANY)] * 5
                + [tok(W_AB), tok(1024), small(), small(),
                   pl.BlockSpec((4, DIFF_DQK), lambda i, s, pt: (0, 0)),
                   pl.BlockSpec((1, 1024), lambda i, s, pt: (0, 0))])
    rows = 3 * N_HEADS * NQ
    page_buf = lambda: pltpu.VMEM((2, g_pages, N_HEADS, HEAD_V, PAGE), F32)
    return pl.pallas_call(
        functools.partial(_dec_kernel, layer=layer, lam_init=lam_init, past_len=past_len, n_steps=n_steps),
        grid_spec=pltpu.PrefetchScalarGridSpec(
            num_scalar_prefetch=1,
            grid=(nb, n_steps),
            in_specs=in_specs,
            out_specs=pl.BlockSpec((NQ, 2 * GROUP_W), lambda i, s, pt: (i, 0)),
            scratch_shapes=[pltpu.VMEM((2 * N_HEADS * NQ, GROUP_W), BF16),
                            pltpu.VMEM((N_HEADS * NQ, GROUP_W), BF16),
                            pltpu.VMEM((rows, 1), F32), pltpu.VMEM((rows, 1), F32),
                            pltpu.VMEM((rows, GROUP_W), F32),
                            pltpu.VMEM((8, 1), F32), pltpu.VMEM((N_HEADS * NQ, LANE), F32),
                            page_buf(), page_buf(), page_buf(), page_buf(),
                            pltpu.VMEM((2, g_pages, N_HEADS, PAGE), F32),
                            pltpu.SemaphoreType.DMA((2, 5, g_pages))]),
        out_shape=jax.ShapeDtypeStruct((nb * NQ, 2 * GROUP_W), F32),
        compiler_params=_params(("arbitrary", "arbitrary")),
        name="attn_sample",
    )(page_table, ck_a, cv_a, ck_b, cv_b, c_lf, qkv, kvn, lf_row, lf_col, lam_params_l, g_out_l)


def kernel(x_prompt, x_sample, cache_a_k, cache_a_v, cache_b_k, cache_b_v, cache_b_logf, state_c, state_d,
           page_table, meta_tokens, w_in, b_f, w_c_gate, b_c_gate, lam_params, g_out, w_out, ln1_g, ln1_b,
           w_ffn_in, w_ffn_out, ln2_g, ln2_b):
    depth = w_in.shape[0]
    alpha = (2.0 * depth) ** 0.25
    nb, seq, _ = x_prompt.shape
    db, nq, _ = x_sample.shape
    assert nq == NQ
    l_true = N_META + seq
    lp = -(-l_true // BLK) * BLK
    xp = jnp.concatenate([jnp.broadcast_to(meta_tokens.astype(F32), (nb, N_META, D_MODEL)), x_prompt,
                          jnp.zeros((nb, lp - l_true, D_MODEL), F32)], axis=1).reshape(nb * lp, D_MODEL)
    xs = x_sample.reshape(db * nq, D_MODEL)
    tm_p = 512 if (nb * lp) % 512 == 0 else BLK
    tm_s = db * nq

    tr = lambda c: jnp.transpose(c, (0, 1, 3, 4, 2))
    caches = (tr(cache_a_k), tr(cache_a_v), tr(cache_b_k), tr(cache_b_v), jnp.transpose(cache_b_logf, (0, 1, 3, 2)))

    w_all = _prep_w_in(jnp.transpose(w_in, (0, 2, 1)))
    kvs, lgs, p_states, s_rows = [], [], [], []
    for l in range(depth):
        lam_init = 0.8 - 0.6 * math.exp(-0.3 * l)
        w = w_all[l]
        bf = jnp.pad(b_f[l], (0, LANE - N_HEADS)).reshape(1, LANE)
        wcg = jnp.pad(w_c_gate[l], ((0, LANE - GLA_RANK), (0, 0)))
        bcg = b_c_gate[l].reshape(1, LANE)
        gl = g_out[l].reshape(1, 1024)
        wo = w_out[l].astype(BF16)
        wfi = w_ffn_in[l].astype(BF16)
        wfo = w_ffn_out[l].astype(BF16)
        vec = lambda a: a[l].reshape(1, D_MODEL)
        post = functools.partial(_post, wo=wo, g1=vec(ln1_g), b1=vec(ln1_b), wfi=wfi, wfo=wfo,
                                 g2=vec(ln2_g), b2=vec(ln2_b), alpha=alpha)

        qkv, kv, cd, lg, vt = _proj(xp, w, bf, wcg, bcg, tm_p, BF16, LOG2E)
        lg3 = lg.reshape(nb, lp, 2 * LANE)
        kvs.append(kv.reshape(nb, lp, 1024))
        lgs.append(lg3)
        cab, *row_outs = _attn_prompt(qkv.reshape(nb, lp, W_AB), vt, lg3, lam_params[l], gl, lam_init,
                                      (kvs, lgs) if l == depth - 1 else None, l_true)
        zero_state = jnp.zeros((nb, LANE, HEAD_V), F32)
        ccd, sc_p, sd_p = _lin(cd.reshape(nb, lp, W_CD), lg3, zero_state, zero_state, gl, l_true, BF16)
        xp = post(xp, cab.reshape(nb * lp, 512), ccd.reshape(nb * lp, 512), tm=tm_p)
        p_states.append((sc_p.reshape(nb, N_HEADS, LIN_DK, HEAD_V), sd_p.reshape(nb, N_HEADS, LIN_DK, HEAD_V)))

        qkv_s, kv_s, cd_s, lg_s, _ = _proj(xs, w, bf, wcg, bcg, tm_s, F32, 1.0)
        lf_s = lg_s[:, 0:N_HEADS].reshape(db, nq, N_HEADS)
        lf_col = jnp.pad(lf_s, ((0, 0), (0, 0), (0, LANE - N_HEADS)))
        lf_row = jnp.pad(jnp.transpose(lf_s, (0, 2, 1)), ((0, 0), (0, 8 - N_HEADS), (0, LANE - nq)))
        cab_s = _attn_sample(page_table, caches, l, qkv_s, kv_s, lf_row, lf_col, lam_params[l], gl, lam_init)
        pad_rows = lambda a: jnp.pad(a.reshape(db, nq, a.shape[-1]), ((0, 0), (0, BLK - nq), (0, 0)))
        ccd_s, sc_s, sd_s = _lin(pad_rows(cd_s), pad_rows(lg_s), state_c[l].reshape(db, LANE, HEAD_V),
                                 state_d[l].reshape(db, LANE, HEAD_V), gl, nq, F32)
        xs = post(xs, cab_s, ccd_s[:, 0:nq].reshape(db * nq, 512), tm=tm_s)
        heads_s = lambda a: a.reshape(db, nq, N_HEADS, HEAD_V)
        s_rows.append((heads_s(kv_s[:, 0:256]), heads_s(kv_s[:, 256:512]), heads_s(kv_s[:, 512:768]),
                       heads_s(kv_s[:, 768:1024]), lf_s,
                       sc_s.reshape(db, N_HEADS, LIN_DK, HEAD_V), sd_s.reshape(db, N_HEADS, LIN_DK, HEAD_V)))

    y_prompt = xp.reshape(nb, lp, D_MODEL)[:, N_META:l_true]
    y_sample = xs.reshape(db, nq, D_MODEL)
    ka_t, va_t, kb_t, vb_t, lf_t = row_outs
    heads_t = lambda a: jnp.transpose(a.reshape(depth, nb, N_HEADS, HEAD_V, l_true), (0, 1, 4, 2, 3))
    p_out = [heads_t(ka_t), heads_t(va_t), heads_t(kb_t), heads_t(vb_t), jnp.transpose(lf_t, (0, 1, 3, 2))]
    p_out += [jnp.stack(z) for z in zip(*p_states)]
    s_out = [jnp.stack(z) for z in zip(*s_rows)]
    return (y_prompt, y_sample, *p_out, *s_out)
```

```python
import functools
import math

import jax
import jax.numpy as jnp
from jax import lax
from jax.experimental import pallas as pl
from jax.experimental.pallas import tpu as pltpu

F32 = jnp.float32
BF16 = jnp.bfloat16

D_MODEL = 1024
N_META = 16
N_HEADS = 4
HEAD_V = 64
GROUP_W = 256
DIFF_DQK = 32
LIN_DK = 32
GLA_RANK = 16
GLA_TAU = 16.0
D_FF = 2816
EPS = 1e-5
NEG = -1e30
PAGE = 128
BLK = 128
LANE = 128
A_SCALE = DIFF_DQK ** -0.5
B_SCALE = HEAD_V ** -0.5
LIN_SCALE = LIN_DK ** -0.5
SLOPES = tuple(2.0 ** (-8.0 * (h + 1.0) / N_HEADS) for h in range(N_HEADS))
LOG_GAMMA = tuple(math.log1p(-(2.0 ** (-5.0 - h))) for h in range(N_HEADS))
LOG2E = math.log2(math.e)
IN_SIZES = (256, 256, 256, 256, 256, 256, 4, 128, 128, 256, 256, 16, 128, 128, 256, 256)
VMEM_LIMIT = 56 * 1024 * 1024


def _dot(a, b):
    return jnp.dot(a, b, preferred_element_type=F32)


def _dot_nt(a, b):
    return lax.dot_general(a, b, (((1,), (1,)), ((), ())), preferred_element_type=F32)


def _split3(x):
    hi = x.astype(BF16)
    r = x - hi.astype(F32)
    mid = r.astype(BF16)
    lo = (r - mid.astype(F32)).astype(BF16)
    return hi, mid, lo


def _dot_lhs3(x, m):
    hi, mid, lo = _split3(x)
    return _dot(hi, m) + _dot(mid, m) + _dot(lo, m)


def _dot_rhs3(m, x):
    hi, mid, lo = _split3(x)
    return _dot(m, hi) + _dot(m, mid) + _dot(m, lo)


def _iota(shape, dim):
    return lax.broadcasted_iota(jnp.int32, shape, dim)


def _tri_incl(n):
    return jnp.where(_iota((n, n), 0) >= _iota((n, n), 1), 1.0, 0.0).astype(BF16)


def _seg_mean_matrix(width, seg):
    same = (_iota((width, width), 0) // seg) == (_iota((width, width), 1) // seg)
    return jnp.where(same, 1.0 / seg, 0.0).astype(BF16)


def _block_diag_mask():
    r = _iota((LANE, GROUP_W), 0) // LIN_DK
    c = _iota((LANE, GROUP_W), 1) // HEAD_V
    return jnp.where(r == c, 1.0, 0.0).astype(F32)


def _log_sigmoid(x):
    return jnp.minimum(x, 0.0) - jnp.log1p(jnp.exp(-jnp.abs(x)))


def _silu(x):
    return x / (1.0 + jnp.exp(-x))


def _head_rms(o, mavg, gain):
    ms = _dot_lhs3(o * o, mavg)
    return o * lax.rsqrt(ms + EPS) * gain


def _layer_norm(x, g, b):
    mu = jnp.mean(x, axis=-1, keepdims=True)
    xc = x - mu
    var = jnp.mean(xc * xc, axis=-1, keepdims=True)
    return xc * lax.rsqrt(var + EPS) * g + b


def _lam(lp_ref, lam_init):
    lp = lp_ref[...]
    a = jnp.sum(lp[0:1, :] * lp[1:2, :], axis=1, keepdims=True)
    b = jnp.sum(lp[2:3, :] * lp[3:4, :], axis=1, keepdims=True)
    return jnp.exp(a) - jnp.exp(b) + lam_init


def _params(sem):
    return pltpu.CompilerParams(dimension_semantics=sem, vmem_limit_bytes=VMEM_LIMIT)


def _const_spec(shape):
    n = len(shape)
    return pl.BlockSpec(shape, lambda *_: (0,) * n, pipeline_mode=pl.Buffered(1))


W_AB = 1536
W_CD = 1536
W_FG = 256
W_ALL = W_AB + W_CD + W_FG


def _prep_w_in(w_in_t):
    offs = [0]
    for s in IN_SIZES:
        offs.append(offs[-1] + s)
    seg = lambda i: w_in_t[:, offs[i]:offs[i + 1], :]
    pad = lambda a: jnp.pad(a, ((0, 0), (0, LANE - a.shape[1]), (0, 0)))
    rows = [seg(i) for i in (0, 1, 2, 3, 4, 5)] + [seg(i) for i in (7, 8, 9, 10, 12, 13, 14, 15)]
    rows += [pad(seg(6)), pad(seg(11))]
    return jnp.concatenate(rows, axis=1).astype(BF16)


def _proj_kernel(x_ref, w_ref, bf_ref, wcg_ref, bcg_ref, qkv_ref, kv_ref, cd_ref, lg_ref, vt_ref, *, q_scale):
    xb = x_ref[...].astype(BF16)
    ab = _dot_nt(xb, w_ref[0:W_AB, :])
    dt = qkv_ref.dtype
    qkv_ref[:, 0:256] = (ab[:, 0:256] * (A_SCALE * q_scale)).astype(dt)
    qkv_ref[:, 256:768] = ab[:, 256:768].astype(dt)
    qkv_ref[:, 768:1024] = (ab[:, 768:1024] * (B_SCALE * q_scale)).astype(dt)
    qkv_ref[:, 1024:1536] = ab[:, 1024:1536].astype(dt)
    kv_ref[:, 0:512] = ab[:, 256:768]
    kv_ref[:, 512:1024] = ab[:, 1024:1536]
    vt_ref[0:GROUP_W, :] = ab[:, 512:768].T.astype(BF16)
    vt_ref[GROUP_W:2 * GROUP_W, :] = ab[:, 1280:1536].T.astype(BF16)
    cd_ref[...] = _dot_nt(xb, w_ref[W_AB:W_AB + W_CD, :])
    fa = _dot_nt(xb, w_ref[W_AB + W_CD:W_ALL, :])
    lg_ref[:, 0:LANE] = _log_sigmoid(fa[:, 0:LANE] + bf_ref[...])
    a = fa[:, LANE:2 * LANE]
    a_hi = a.astype(BF16)
    a_lo = (a - a_hi.astype(F32)).astype(BF16)
    w = wcg_ref[...]
    w_hi = w.astype(BF16)
    w_lo = (w - w_hi.astype(F32)).astype(BF16)
    z = _dot(a_hi, w_hi) + _dot(a_hi, w_lo) + _dot(a_lo, w_hi) + bcg_ref[...]
    lg_ref[:, LANE:2 * LANE] = _log_sigmoid(z) / GLA_TAU


def _proj(x, w, bf, wcg, bcg, tm, qkv_dtype, q_scale):
    t = x.shape[0]
    row = lambda n: pl.BlockSpec((tm, n), lambda i: (i, 0))
    return pl.pallas_call(
        functools.partial(_proj_kernel, q_scale=q_scale),
        grid=(t // tm,),
        in_specs=[row(D_MODEL), _const_spec((W_ALL, D_MODEL)), _const_spec((1, LANE)),
                  _const_spec((LANE, LANE)), _const_spec((1, LANE))],
        out_specs=[row(W_AB), row(1024), row(W_CD), row(2 * LANE),
                   pl.BlockSpec((2 * GROUP_W, tm), lambda i: (0, i))],
        out_shape=[jax.ShapeDtypeStruct((t, W_AB), qkv_dtype),
                   jax.ShapeDtypeStruct((t, 1024), F32),
                   jax.ShapeDtypeStruct((t, W_CD), F32),
                   jax.ShapeDtypeStruct((t, 2 * LANE), F32),
                   jax.ShapeDtypeStruct((2 * GROUP_W, t), BF16)],
        compiler_params=_params(("parallel",)),
        name="proj",
    )(x, w, bf, wcg, bcg)


def _emit_rows(kv_refs, lg_refs, out_refs):
    ka_ref, va_ref, kb_ref, vb_ref, lf_ref = out_refs
    for l in range(len(kv_refs)):
        kv = kv_refs[l][0]
        ka_ref[l, 0] = kv[:, 0:256].T
        va_ref[l, 0] = kv[:, 256:512].T
        kb_ref[l, 0] = kv[:, 512:768].T
        vb_ref[l, 0] = kv[:, 768:1024].T
        lf_ref[l, 0] = lg_refs[l][0].T[0:N_HEADS, :]


BIAS_F0 = 3 * N_HEADS


A_W = 4 * BLK
B_W = 2 * BLK
S_W = 2 * A_W + 2 * B_W


def _map_lanes(h, m):
    j, half = divmod(h, 2)
    if m < 2:
        return j * A_W + (2 * half + m) * BLK
    return 2 * A_W + j * B_W + half * BLK


def _attn_kernel(*refs, lam_init, nblk, n_rows):
    qa_ref, ka_ref, qb_ref, kb_ref, vt_ref, lf_ref, lp_ref, g_ref = refs[0:8]
    n_in = 8 + 2 * n_rows
    o_ref = refs[n_in]
    kbias, wqa, wqb, m_ref, l_ref, acc_ref, s_even, s_odd = refs[-8:]
    if n_rows:
        _emit_rows(refs[8:8 + n_rows], refs[8 + n_rows:n_in], refs[n_in + 1:n_in + 6])
    qi = pl.program_id(1)
    row = _iota((BLK, BLK), 0)
    col = _iota((BLK, BLK), 1)

    @pl.when(qi == 0)
    def _():
        tri = _tri_incl(BLK)
        sel = [[jnp.where((row < N_HEADS) & (col == base + 3 * row + i), 1.0, 0.0).astype(BF16)
                for i in range(3)] for base in (0, BIAS_F0)]
        slope = _lane_const(tuple(LOG2E * s for s in SLOPES) + (0.0,), LANE, 1)
        carry = jnp.zeros((1, LANE), F32)
        for i in range(nblk):
            cs = _dot_rhs3(tri, lf_ref[0, i * BLK:(i + 1) * BLK, :]) + carry
            carry = cs[BLK - 1:BLK, :]
            pos = (i * BLK + row).astype(F32)
            acc = jnp.zeros((BLK, LANE), F32)
            for terms, mats in ((_split3(slope * pos), sel[0]), (_split3(cs * -LOG2E), sel[1])):
                for t, mat in zip(terms, mats):
                    acc = acc + _dot(t, mat)
            kbias[i * BLK:(i + 1) * BLK, :] = acc.astype(BF16)
        for h in range(N_HEADS):
            j, half = divmod(h, 2)
            w_alibi = jnp.where((row >= 3 * h) & (row < 3 * h + 3), 1.0, 0.0).astype(BF16)
            w_fox = jnp.where((row >= BIAS_F0 + 3 * h) & (row < BIAS_F0 + 3 * h + 3), 1.0, 0.0).astype(BF16)
            for m in range(2):
                c0 = (2 * half + m) * BLK
                wqa[j, LANE:2 * LANE, c0:c0 + BLK] = w_alibi
            wqb[j, LANE:2 * LANE, half * BLK:(half + 1) * BLK] = w_fox

    m_ref[...] = jnp.full(m_ref.shape, NEG, F32)
    l_ref[...] = jnp.zeros(l_ref.shape, F32)
    acc_ref[...] = jnp.zeros(acc_ref.shape, F32)

    qa = qa_ref[0].astype(F32)
    qb = qb_ref[0].astype(F32)
    for j in range(2):
        qa_t = qa[:, j * LANE:(j + 1) * LANE].T
        qb_t = qb[:, j * LANE:(j + 1) * LANE].T
        for half in range(2):
            lo = half * HEAD_V
            for m in range(2):
                keep = (row >= lo + m * DIFF_DQK) & (row < lo + (m + 1) * DIFF_DQK)
                c0 = (2 * half + m) * BLK
                wqa[j, 0:LANE, c0:c0 + BLK] = jnp.where(keep, qa_t, 0.0).astype(BF16)
            keep = (row >= lo) & (row < lo + HEAD_V)
            wqb[j, 0:LANE, half * BLK:(half + 1) * BLK] = jnp.where(keep, qb_t, 0.0).astype(BF16)

    def scores(kj):
        ks = pl.multiple_of(kj * BLK, BLK)
        bias = kbias[pl.ds(ks, BLK), :]
        parts = []
        for k_ref, w_ref in ((ka_ref, wqa), (kb_ref, wqb)):
            for j in range(2):
                kk = jnp.concatenate([k_ref[0, pl.ds(ks, BLK), j * LANE:(j + 1) * LANE], bias], axis=1)
                parts.append(_dot(kk, w_ref[j]))
        return jnp.concatenate(parts, axis=1)

    def update(kj, s):
        ks = pl.multiple_of(kj * BLK, BLK)
        m_prev = m_ref[...]
        m_new = jnp.maximum(m_prev, jnp.max(s, axis=0, keepdims=True))
        alpha = jnp.exp2(m_prev - m_new)
        p = jnp.exp2(s - m_new)
        l_ref[...] = alpha * l_ref[...] + jnp.sum(p, axis=0, keepdims=True)
        m_ref[...] = m_new
        pb = p.astype(BF16)
        for h in range(N_HEADS):
            vta = vt_ref[h * HEAD_V:(h + 1) * HEAD_V, pl.ds(ks, BLK)]
            vtb = vt_ref[GROUP_W + h * HEAD_V:GROUP_W + (h + 1) * HEAD_V, pl.ds(ks, BLK)]
            for c0, w, vt in ((_map_lanes(h, 0), 2 * BLK, vta), (_map_lanes(h, 2), BLK, vtb)):
                acc_ref[:, c0:c0 + w] = alpha[:, c0:c0 + w] * acc_ref[:, c0:c0 + w] + _dot(vt, pb[:, c0:c0 + w])

    def diag(s):
        return jnp.where(_iota((BLK, S_W), 0) <= _iota((BLK, S_W), 1) % BLK, s, NEG)

    s_even[...] = scores(0)

    def pair(i, carry):
        s_odd[...] = scores(2 * i + 1)
        update(2 * i, s_even[...])
        s_even[...] = scores(2 * i + 2)
        update(2 * i + 1, s_odd[...])
        return carry

    lax.fori_loop(0, qi // 2, pair, 0)

    @pl.when(qi % 2 == 0)
    def _():
        update(qi, diag(s_even[...]))

    @pl.when(qi % 2 == 1)
    def _():
        s_odd[...] = scores(qi)
        update(qi - 1, s_even[...])
        update(qi, diag(s_odd[...]))

    lam = _lam(lp_ref, lam_init)
    g = g_ref[...]
    inv_l = 1.0 / l_ref[...]

    def head_out(h, m):
        c0 = _map_lanes(h, m)
        return acc_ref[:, c0:c0 + BLK] * inv_l[:, c0:c0 + BLK]

    def rms_t(o):
        return o * lax.rsqrt(jnp.mean(o * o, axis=0, keepdims=True) + EPS)

    for j in range(2):
        oa, ob = [], []
        for half in range(2):
            h = 2 * j + half
            oa.append(rms_t(head_out(h, 0) - lam * head_out(h, 1)))
            ob.append(rms_t(head_out(h, 2)))
        ga = g[:, j * LANE:(j + 1) * LANE] * (1.0 - lam_init)
        gb = g[:, GROUP_W + j * LANE:GROUP_W + (j + 1) * LANE]
        o_ref[0, :, j * LANE:(j + 1) * LANE] = (jnp.concatenate(oa, axis=0).T * ga).astype(o_ref.dtype)
        o_ref[0, :, GROUP_W + j * LANE:GROUP_W + (j + 1) * LANE] = (
            jnp.concatenate(ob, axis=0).T * gb).astype(o_ref.dtype)


def _attn_prompt(qkv, vt, lg, lam_params_l, g_out_l, lam_init, row_src=None, l_true=None):
    b, lp, _ = qkv.shape
    nblk = lp // BLK
    assert nblk <= 256
    qspec = lambda c: pl.BlockSpec((1, BLK, GROUP_W), lambda i, j: (i, j, c))
    kspec = lambda c: pl.BlockSpec((1, lp, GROUP_W), lambda i, j: (i, 0, c))
    kvs, lgs = row_src if row_src else ((), ())
    depth = len(kvs)
    row_in = ([pl.BlockSpec((1, BLK, 1024), lambda i, j: (i, j, 0))] * depth
              + [pl.BlockSpec((1, BLK, LANE), lambda i, j: (i, j, 0))] * depth)
    row_out, row_shape = [], []
    if depth:
        row_out = [pl.BlockSpec((depth, 1, GROUP_W, BLK), lambda i, j: (0, i, 0, j))] * 4
        row_out += [pl.BlockSpec((depth, 1, N_HEADS, BLK), lambda i, j: (0, i, 0, j))]
        row_shape = [jax.ShapeDtypeStruct((depth, b, GROUP_W, l_true), F32)] * 4
        row_shape += [jax.ShapeDtypeStruct((depth, b, N_HEADS, l_true), F32)]
    return pl.pallas_call(
        functools.partial(_attn_kernel, lam_init=lam_init, nblk=nblk, n_rows=depth),
        grid=(b, nblk),
        in_specs=[qspec(0), kspec(1), qspec(3), kspec(4),
                  pl.BlockSpec((2 * GROUP_W, lp), lambda i, j: (0, i)),
                  pl.BlockSpec((1, lp, LANE), lambda i, j: (i, 0, 0)),
                  _const_spec((4, DIFF_DQK)), _const_spec((1, 1024))] + row_in,
        out_specs=[pl.BlockSpec((1, BLK, 2 * GROUP_W), lambda i, j: (i, j, 0))] + row_out,
        out_shape=[jax.ShapeDtypeStruct((b, lp, 2 * GROUP_W), BF16)] + row_shape,
        scratch_shapes=[pltpu.VMEM((lp, LANE), BF16),
                        pltpu.VMEM((2, 2 * LANE, A_W), BF16), pltpu.VMEM((2, 2 * LANE, B_W), BF16),
                        pltpu.VMEM((1, S_W), F32), pltpu.VMEM((1, S_W), F32), pltpu.VMEM((HEAD_V, S_W), F32),
                        pltpu.VMEM((BLK, S_W), F32), pltpu.VMEM((BLK, S_W), F32)],
        compiler_params=_params(("parallel", "arbitrary")),
        name="attn_prompt",
    )(qkv, qkv, qkv, qkv, vt, lg, lam_params_l, g_out_l, *kvs, *lgs)


SUB = 32


def _lane_const(values, width, seg):
    lane = _iota((1, width), 1) // seg
    out = jnp.full((1, width), values[-1], F32)
    for h in range(len(values) - 2, -1, -1):
        out = jnp.where(lane == h, values[h], out)
    return out


def _state_rows(s_bd):
    hsum = s_bd[:, 0:LANE] + s_bd[:, LANE:2 * LANE]
    return (hsum + pltpu.roll(hsum, HEAD_V, 1))[:, 0:HEAD_V]


def _lin_kernel(cd_ref, g_ref, sc0_ref, sd0_ref, gout_ref, cat_ref, sc_ref, sd_ref, s_c, s_d, o_s, dec_s,
                *, valid_len, n_tiles, nchunks):
    c = pl.program_id(1)
    bd = _block_diag_mask()
    e2 = bd.astype(BF16)

    @pl.when(c == 0)
    def _():
        expand = jnp.where(_iota((HEAD_V, GROUP_W), 0) == _iota((HEAD_V, GROUP_W), 1) % HEAD_V,
                           1.0, 0.0).astype(BF16)
        s_c[...] = _dot_lhs3(sc0_ref[0], expand) * bd
        s_d[...] = _dot_lhs3(sd0_ref[0], expand) * bd
        rr = _iota((BLK, BLK), 0)
        cc = _iota((BLK, BLK), 1)
        dist = (rr - cc).astype(F32)
        for h in range(N_HEADS):
            dec_s[h] = jnp.exp(jnp.where(rr >= cc, dist * LOG_GAMMA[h], NEG))

    rowi = _iota((BLK, 1), 0)
    valid = (c * BLK + rowi) < valid_len
    n_v = jnp.clip(valid_len - c * BLK, 0, BLK).astype(F32)
    x = cd_ref[0]
    g = jnp.where(valid, g_ref[0], 0.0)
    q_c = x[:, 0:128] * LIN_SCALE
    k_c = jnp.where(valid, x[:, 128:256], 0.0)
    v_c = x[:, 256:512]
    r_c = x[:, 512:768]
    q_d = x[:, 768:896]
    k_d = jnp.where(valid, x[:, 896:1024], 0.0) * LIN_SCALE
    v_d = x[:, 1024:1280]
    r_d = x[:, 1280:1536]

    b = _dot_rhs3(_tri_incl(BLK), g)
    b_last = b[BLK - 1:BLK, :]
    s_prev = s_c[...]
    o_s[...] = _dot((q_c * jnp.exp(b)).astype(BF16), s_prev.astype(BF16))
    lane_h = _iota((SUB, LANE), 1) // LIN_DK
    col_h = _iota((SUB, GROUP_W), 1) // HEAD_V
    vcb = v_c.astype(BF16)
    for sb in range(-(-n_tiles * 8 // SUB)):
        r0 = sb * SUB
        for ti in range(min(SUB // 8, n_tiles - r0 // 8)):
            t0 = r0 + 8 * ti
            ns = 8 * (ti + 1)
            ws = []
            for r in range(8):
                t = t0 + r
                diff = b[t:t + 1, :] - b[r0:r0 + ns, :]
                dec = jnp.exp(jnp.where(rowi[r0:r0 + ns] <= t, diff, NEG))
                ws.append(dec * (k_c[r0:r0 + ns] * q_c[t:t + 1, :]))
            att = _dot(jnp.concatenate(ws, axis=0).astype(BF16), e2)
            o_t = jnp.sum(att.reshape(8, ns, GROUP_W) * v_c[r0:r0 + ns][None], axis=1)
            o_s[t0:t0 + 8, :] = o_s[t0:t0 + 8, :] + o_t
        if sb > 0:
            ref = b[r0 - 1:r0, :]
            qt = q_c[r0:r0 + SUB] * jnp.exp(b[r0:r0 + SUB] - ref)
            ke = (k_c[0:r0] * jnp.exp(ref - b[0:r0])).astype(BF16)
            qs = jnp.concatenate([jnp.where(lane_h == h, qt, 0.0) for h in range(N_HEADS)], axis=0)
            full = _dot(_dot_nt(qs.astype(BF16), ke).astype(BF16), vcb[0:r0])
            o_off = jnp.zeros((SUB, GROUP_W), F32)
            for h in range(N_HEADS):
                o_off = o_off + jnp.where(col_h == h, full[h * SUB:(h + 1) * SUB], 0.0)
            o_s[r0:r0 + SUB, :] = o_s[r0:r0 + SUB, :] + o_off
    o_c = o_s[...]
    kt = k_c * jnp.exp(b_last - b)
    d_col = jnp.broadcast_to(jnp.exp(b_last), (LANE, LANE)).T
    s_new = s_prev * jnp.concatenate([d_col, d_col], axis=1) + _dot(kt.T.astype(BF16), v_c.astype(BF16)) * bd
    s_c[...] = s_new

    lgl = _lane_const(LOG_GAMMA, LANE, LIN_DK)
    rowf = rowi.astype(F32)
    d_prev = s_d[...]
    o_d = _dot((q_d * jnp.exp((rowf + 1.0) * lgl)).astype(BF16), d_prev.astype(BF16))
    lane_h = _iota((BLK, LANE), 1) // LIN_DK
    col_h = _iota((BLK, GROUP_W), 1) // HEAD_V
    kdb = k_d.astype(BF16)
    vdb = v_d.astype(BF16)
    for h in range(N_HEADS):
        qm = jnp.where(lane_h == h, q_d, 0.0).astype(BF16)
        att = (_dot_nt(qm, kdb) * dec_s[h]).astype(BF16)
        o_d = o_d + jnp.where(col_h == h, _dot(att, vdb), 0.0)
    deck = jnp.where(valid, jnp.exp((n_v - 1.0 - rowf) * lgl), 0.0)
    lgr = jnp.full((LANE, 1), LOG_GAMMA[-1], F32)
    rrow = _iota((LANE, 1), 0) // LIN_DK
    for h in range(N_HEADS - 2, -1, -1):
        lgr = jnp.where(rrow == h, LOG_GAMMA[h], lgr)
    d_new = d_prev * jnp.exp(n_v * lgr) + _dot((k_d * deck).T.astype(BF16), vdb) * bd
    s_d[...] = d_new

    mavg = _seg_mean_matrix(GROUP_W, HEAD_V)
    gout = gout_ref[...]
    cat_ref[0, :, 0:GROUP_W] = (_silu(r_c) * _head_rms(o_c, mavg, gout[:, 512:768])).astype(cat_ref.dtype)
    cat_ref[0, :, GROUP_W:2 * GROUP_W] = (_silu(r_d) * _head_rms(o_d, mavg, gout[:, 768:1024])).astype(cat_ref.dtype)

    @pl.when(c == nchunks - 1)
    def _():
        sc_ref[0] = _state_rows(s_new)
        sd_ref[0] = _state_rows(d_new)


def _lin(cd, lg, sc0, sd0, g_out_l, valid_len, cat_dtype):
    b, lp, _ = cd.shape
    nchunks = lp // BLK
    n_tiles = BLK // 8 if nchunks > 1 else -(-valid_len // 8)
    st = pl.BlockSpec((1, LANE, HEAD_V), lambda i, j: (i, 0, 0))
    return pl.pallas_call(
        functools.partial(_lin_kernel, valid_len=valid_len, n_tiles=n_tiles, nchunks=nchunks),
        grid=(b, nchunks),
        in_specs=[pl.BlockSpec((1, BLK, W_CD), lambda i, j: (i, j, 0)),
                  pl.BlockSpec((1, BLK, LANE), lambda i, j: (i, j, 1)),
                  st, st, _const_spec((1, 1024))],
        out_specs=[pl.BlockSpec((1, BLK, 2 * GROUP_W), lambda i, j: (i, j, 0)), st, st],
        out_shape=[jax.ShapeDtypeStruct((b, lp, 2 * GROUP_W), cat_dtype),
                   jax.ShapeDtypeStruct((b, LANE, HEAD_V), F32),
                   jax.ShapeDtypeStruct((b, LANE, HEAD_V), F32)],
        scratch_shapes=[pltpu.VMEM((LANE, GROUP_W), F32), pltpu.VMEM((LANE, GROUP_W), F32),
                        pltpu.VMEM((BLK, GROUP_W), F32), pltpu.VMEM((N_HEADS, BLK, BLK), F32)],
        compiler_params=_params(("parallel", "arbitrary")),
        name="lin",
    )(cd, lg, sc0, sd0, g_out_l)


FF_CHUNK = 256


def _post_kernel(x_ref, cab_ref, ccd_ref, wo_ref, g1_ref, b1_ref, wfi_ref, wfo_ref, g2_ref, b2_ref, o_ref,
                 act_ref, *, alpha):
    y = (_dot(cab_ref[...].astype(BF16), wo_ref[0:512, :])
         + _dot(ccd_ref[...].astype(BF16), wo_ref[512:1024, :]))
    x1 = _layer_norm(alpha * x_ref[...] + y, g1_ref[...], b1_ref[...])
    x1b = x1.astype(BF16)
    for j in range(D_FF // FF_CHUNK):
        gate = _dot(x1b, wfi_ref[:, j * FF_CHUNK:(j + 1) * FF_CHUNK])
        up = _dot(x1b, wfi_ref[:, D_FF + j * FF_CHUNK:D_FF + (j + 1) * FF_CHUNK])
        act_ref[:, j * FF_CHUNK:(j + 1) * FF_CHUNK] = (_silu(gate) * up).astype(BF16)
    y2 = _dot(act_ref[...], wfo_ref[...])
    o_ref[...] = _layer_norm(alpha * x1 + y2, g2_ref[...], b2_ref[...])


def _post(x, cab, ccd, wo, g1, b1, wfi, wfo, g2, b2, tm, alpha):
    t = x.shape[0]
    row = lambda n: pl.BlockSpec((tm, n), lambda i: (i, 0))
    vec = _const_spec((1, D_MODEL))
    return pl.pallas_call(
        functools.partial(_post_kernel, alpha=alpha),
        grid=(t // tm,),
        in_specs=[row(D_MODEL), row(512), row(512), _const_spec((D_MODEL, D_MODEL)), vec, vec,
                  _const_spec((D_MODEL, 2 * D_FF)), _const_spec((D_FF, D_MODEL)), vec, vec],
        out_specs=row(D_MODEL),
        out_shape=jax.ShapeDtypeStruct((t, D_MODEL), F32),
        scratch_shapes=[pltpu.VMEM((tm, D_FF), BF16)],
        compiler_params=_params(("parallel",)),
        name="post",
    )(x, cab, ccd, wo, g1, b1, wfi, wfo, g2, b2)


PAGES_PER_STEP = 16
NQ = 8


def _rep_rows(x4):
    n = x4.shape[1]
    return jnp.concatenate([jnp.broadcast_to(x4[h:h + 1, :], (NQ, n)) for h in range(N_HEADS)], axis=0)


def _dec_kernel(pt_ref, ka_hbm, va_hbm, kb_hbm, vb_hbm, lf_hbm, q_ref, kvn_ref, lfr_ref, lfc_ref, lp_ref, g_ref,
                o_ref, qa_s, qb_s, m_s, l_s, acc_s, carry_s, nq_s, ka_buf, va_buf, kb_buf, vb_buf, lf_buf, sem,
                *, layer, lam_init, past_len, n_steps):
    g_pages = PAGES_PER_STEP
    seq = pl.program_id(0)
    step = pl.program_id(1)
    n_pages = n_steps * g_pages
    width = g_pages * PAGE

    hbms = (ka_hbm, va_hbm, kb_hbm, vb_hbm, lf_hbm)
    bufs = (ka_buf, va_buf, kb_buf, vb_buf, lf_buf)
    t = seq * n_steps + step
    slot = t % 2

    def page_copies(b_idx, s_idx, sl, known_pages):
        out = []
        for r in range(g_pages):
            page = 0 if known_pages else pt_ref[b_idx, n_pages - 1 - (s_idx * g_pages + r)]
            for c in range(len(hbms)):
                out.append(pltpu.make_async_copy(hbms[c].at[layer, page], bufs[c].at[sl, r], sem.at[sl, c, r]))
        return out

    @pl.when(t == 0)
    def _():
        for cp in page_copies(0, 0, 0, False):
            cp.start()

    @pl.when(t + 1 < pl.num_programs(0) * n_steps)
    def _():
        last = step + 1 == n_steps
        for cp in page_copies(jnp.where(last, seq + 1, seq), jnp.where(last, 0, step + 1), 1 - slot, False):
            cp.start()

    for cp in page_copies(seq, step, slot, True):
        cp.wait()
    ka_refs, va_refs, kb_refs, vb_refs, lf_refs = [[b.at[slot, r] for r in range(g_pages)] for b in bufs]

    rows_a = 2 * N_HEADS * NQ
    rows_b = N_HEADS * NQ
    ra = _iota((rows_a, 1), 0)
    slope_a = jnp.full((rows_a, 1), SLOPES[-1], F32)
    for h in range(N_HEADS - 2, -1, -1):
        slope_a = jnp.where(ra // (2 * NQ) == h, SLOPES[h], slope_a)
    t_a = (past_len + ra % NQ).astype(F32)

    @pl.when(step == 0)
    def _():
        q = q_ref[...]
        qa = jnp.concatenate([q[:, 0:GROUP_W]] * (2 * N_HEADS), axis=0)
        r = _iota((rows_a, GROUP_W), 0)
        cidx = _iota((rows_a, GROUP_W), 1)
        keep = (cidx // HEAD_V == r // (2 * NQ)) & ((cidx % HEAD_V) // DIFF_DQK == (r // NQ) % 2)
        qa_s[...] = jnp.where(keep, qa, 0.0).astype(BF16)
        qb = jnp.concatenate([q[:, 768:1024]] * N_HEADS, axis=0)
        r = _iota((rows_b, GROUP_W), 0)
        cidx = _iota((rows_b, GROUP_W), 1)
        qb_s[...] = jnp.where(cidx // HEAD_V == r // NQ, qb, 0.0).astype(BF16)
        m_s[...] = jnp.full(m_s.shape, NEG, F32)
        l_s[...] = jnp.zeros(l_s.shape, F32)
        acc_s[...] = jnp.zeros(acc_s.shape, F32)
        carry_s[...] = jnp.zeros(carry_s.shape, F32)
        cum = _dot_rhs3(_tri_incl(NQ), lfc_ref[0])
        nq_s[...] = jnp.concatenate(
            [jnp.broadcast_to(cum[:, h:h + 1], (NQ, LANE)) for h in range(N_HEADS)], axis=0)

    def update(s, va_t, vb_t, nt):
        m_prev = m_s[...]
        m_new = jnp.maximum(m_prev, jnp.max(s, axis=1, keepdims=True))
        alpha = jnp.exp(m_prev - m_new)
        p = jnp.exp(s - m_new)
        l_s[...] = alpha * l_s[...] + jnp.sum(p, axis=1, keepdims=True)
        pb = p.astype(BF16)
        if nt:
            pv_a = _dot_nt(pb[0:rows_a], va_t)
            pv_b = _dot_nt(pb[rows_a:], vb_t)
        else:
            pv_a = _dot(pb[0:rows_a], va_t)
            pv_b = _dot(pb[rows_a:], vb_t)
        acc_s[...] = alpha * acc_s[...] + jnp.concatenate([pv_a, pv_b], axis=0)
        m_s[...] = m_new

    pages = lambda rs: jnp.concatenate([r[...].reshape(GROUP_W, PAGE).astype(BF16) for r in rs], axis=1)
    lane = _iota((1, width), 1)
    pos = ((n_pages - 1 - step * g_pages - lane // PAGE) * PAGE + lane % PAGE).astype(F32)
    strict = jnp.where(_iota((LANE, LANE), 0) > _iota((LANE, LANE), 1), 1.0, 0.0).astype(BF16)
    zrow = jnp.zeros((8 - N_HEADS, PAGE), F32)
    lf = jnp.concatenate([x for r in range(g_pages) for x in (lf_refs[r][...], zrow)], axis=0)
    inside = _dot_lhs3(lf, strict)
    tot = jnp.sum(lf, axis=1, keepdims=True)
    carry = carry_s[...]
    parts = []
    for r in range(g_pages):
        parts.append(inside[8 * r:8 * r + 8, :] + carry)
        carry = carry + tot[8 * r:8 * r + 8, :]
    carry_s[...] = carry
    csuf = jnp.concatenate(parts, axis=1)
    nq = jnp.concatenate([nq_s[...]] * g_pages, axis=1)
    s_a = _dot(qa_s[...], pages(ka_refs)) - slope_a * (t_a - pos)
    s_b = _dot(qb_s[...], pages(kb_refs)) + nq + _rep_rows(csuf)
    update(jnp.concatenate([s_a, s_b], axis=0), pages(va_refs), pages(vb_refs), True)

    @pl.when(step == n_steps - 1)
    def _():
        kvn = kvn_ref[...]
        zpad = jnp.zeros((PAGE - NQ, GROUP_W), F32)
        kn_a = jnp.concatenate([kvn[:, 0:256], zpad], axis=0).astype(BF16)
        vn_a = jnp.concatenate([kvn[:, 256:512], zpad], axis=0).astype(BF16)
        kn_b = jnp.concatenate([kvn[:, 512:768], zpad], axis=0).astype(BF16)
        vn_b = jnp.concatenate([kvn[:, 768:1024], zpad], axis=0).astype(BF16)
        jl = _iota((1, LANE), 1)
        ok_a = jl <= ra % NQ
        rb = _iota((rows_b, 1), 0)
        ok_b = jl <= rb % NQ
        cum_row = _dot_lhs3(lfr_ref[0], jnp.where(_iota((LANE, LANE), 0) <= _iota((LANE, LANE), 1),
                                                  1.0, 0.0).astype(BF16))
        sn_a = _dot_nt(qa_s[...], kn_a) - slope_a * ((ra % NQ).astype(F32) - jl.astype(F32))
        sn_b = _dot_nt(qb_s[...], kn_b) + nq_s[...] - _rep_rows(cum_row)
        update(jnp.concatenate([jnp.where(ok_a, sn_a, NEG), jnp.where(ok_b, sn_b, NEG)], axis=0),
               vn_a, vn_b, False)

        lam = _lam(lp_ref, lam_init)
        acc = acc_s[...] / l_s[...]
        col_h = _iota((NQ, GROUP_W), 1) // HEAD_V
        o_a = jnp.zeros((NQ, GROUP_W), F32)
        o_b = jnp.zeros((NQ, GROUP_W), F32)
        for h in range(N_HEADS):
            r0 = 2 * NQ * h
            o_a = o_a + jnp.where(col_h == h, acc[r0:r0 + NQ] - lam * acc[r0 + NQ:r0 + 2 * NQ], 0.0)
            rb0 = rows_a + NQ * h
            o_b = o_b + jnp.where(col_h == h, acc[rb0:rb0 + NQ], 0.0)
        mavg = _seg_mean_matrix(GROUP_W, HEAD_V)
        g = g_ref[...]
        o_ref[:, 0:GROUP_W] = _head_rms(o_a, mavg, g[:, 0:256] * (1.0 - lam_init))
        o_ref[:, GROUP_W:2 * GROUP_W] = _head_rms(o_b, mavg, g[:, 256:512])


def _attn_sample(page_table, caches, layer, qkv, kvn, lf_row, lf_col, lam_params_l, g_out_l, lam_init):
    ck_a, cv_a, ck_b, cv_b, c_lf = caches
    nb, n_pages = page_table.shape
    g_pages = PAGES_PER_STEP
    assert n_pages % g_pages == 0
    n_steps = n_pages // g_pages
    past_len = n_pages * PAGE

    tok = lambda n: pl.BlockSpec((NQ, n), lambda i, s, pt: (i, 0))
    small = lambda: pl.BlockSpec((1, 8, LANE), lambda i, s, pt: (i, 0, 0))
    in_specs = ([pl.BlockSpec(memory_space=pl.ANY)] * 5
                + [tok(W_AB), tok(1024), small(), small(),
                   pl.BlockSpec((4, DIFF_DQK), lambda i, s, pt: (0, 0)),
                   pl.BlockSpec((1, 1024), lambda i, s, pt: (0, 0))])
    rows = 3 * N_HEADS * NQ
    page_buf = lambda: pltpu.VMEM((2, g_pages, N_HEADS, HEAD_V, PAGE), F32)
    return pl.pallas_call(
        functools.partial(_dec_kernel, layer=layer, lam_init=lam_init, past_len=past_len, n_steps=n_steps),
        grid_spec=pltpu.PrefetchScalarGridSpec(
            num_scalar_prefetch=1,
            grid=(nb, n_steps),
            in_specs=in_specs,
            out_specs=pl.BlockSpec((NQ, 2 * GROUP_W), lambda i, s, pt: (i, 0)),
            scratch_shapes=[pltpu.VMEM((2 * N_HEADS * NQ, GROUP_W), BF16),
                            pltpu.VMEM((N_HEADS * NQ, GROUP_W), BF16),
                            pltpu.VMEM((rows, 1), F32), pltpu.VMEM((rows, 1), F32),
                            pltpu.VMEM((rows, GROUP_W), F32),
                            pltpu.VMEM((8, 1), F32), pltpu.VMEM((N_HEADS * NQ, LANE), F32),
                            page_buf(), page_buf(), page_buf(), page_buf(),
                            pltpu.VMEM((2, g_pages, N_HEADS, PAGE), F32),
                            pltpu.SemaphoreType.DMA((2, 5, g_pages))]),
        out_shape=jax.ShapeDtypeStruct((nb * NQ, 2 * GROUP_W), F32),
        compiler_params=_params(("arbitrary", "arbitrary")),
        name="attn_sample",
    )(page_table, ck_a, cv_a, ck_b, cv_b, c_lf, qkv, kvn, lf_row, lf_col, lam_params_l, g_out_l)


def kernel(x_prompt, x_sample, cache_a_k, cache_a_v, cache_b_k, cache_b_v, cache_b_logf, state_c, state_d,
           page_table, meta_tokens, w_in, b_f, w_c_gate, b_c_gate, lam_params, g_out, w_out, ln1_g, ln1_b,
           w_ffn_in, w_ffn_out, ln2_g, ln2_b):
    depth = w_in.shape[0]
    alpha = (2.0 * depth) ** 0.25
    nb, seq, _ = x_prompt.shape
    db, nq, _ = x_sample.shape
    assert nq == NQ
    l_true = N_META + seq
    lp = -(-l_true // BLK) * BLK
    xp = jnp.concatenate([jnp.broadcast_to(meta_tokens.astype(F32), (nb, N_META, D_MODEL)), x_prompt,
                          jnp.zeros((nb, lp - l_true, D_MODEL), F32)], axis=1).reshape(nb * lp, D_MODEL)
    xs = x_sample.reshape(db * nq, D_MODEL)
    tm_p = 512 if (nb * lp) % 512 == 0 else BLK
    tm_s = db * nq

    tr = lambda c: jnp.transpose(c, (0, 1, 3, 4, 2))
    caches = (tr(cache_a_k), tr(cache_a_v), tr(cache_b_k), tr(cache_b_v), jnp.transpose(cache_b_logf, (0, 1, 3, 2)))

    w_all = _prep_w_in(jnp.transpose(w_in, (0, 2, 1)))
    kvs, lgs, p_states, s_rows = [], [], [], []
    for l in range(depth):
        lam_init = 0.8 - 0.6 * math.exp(-0.3 * l)
        w = w_all[l]
        bf = jnp.pad(b_f[l], (0, LANE - N_HEADS)).reshape(1, LANE)
        wcg = jnp.pad(w_c_gate[l], ((0, LANE - GLA_RANK), (0, 0)))
        bcg = b_c_gate[l].reshape(1, LANE)
        gl = g_out[l].reshape(1, 1024)
        wo = w_out[l].astype(BF16)
        wfi = w_ffn_in[l].astype(BF16)
        wfo = w_ffn_out[l].astype(BF16)
        vec = lambda a: a[l].reshape(1, D_MODEL)
        post = functools.partial(_post, wo=wo, g1=vec(ln1_g), b1=vec(ln1_b), wfi=wfi, wfo=wfo,
                                 g2=vec(ln2_g), b2=vec(ln2_b), alpha=alpha)

        qkv, kv, cd, lg, vt = _proj(xp, w, bf, wcg, bcg, tm_p, BF16, LOG2E)
        lg3 = lg.reshape(nb, lp, 2 * LANE)
        kvs.append(kv.reshape(nb, lp, 1024))
        lgs.append(lg3)
        cab, *row_outs = _attn_prompt(qkv.reshape(nb, lp, W_AB), vt, lg3, lam_params[l], gl, lam_init,
                                      (kvs, lgs) if l == depth - 1 else None, l_true)
        zero_state = jnp.zeros((nb, LANE, HEAD_V), F32)
        ccd, sc_p, sd_p = _lin(cd.reshape(nb, lp, W_CD), lg3, zero_state, zero_state, gl, l_true, BF16)
        xp = post(xp, cab.reshape(nb * lp, 512), ccd.reshape(nb * lp, 512), tm=tm_p)
        p_states.append((sc_p.reshape(nb, N_HEADS, LIN_DK, HEAD_V), sd_p.reshape(nb, N_HEADS, LIN_DK, HEAD_V)))

        qkv_s, kv_s, cd_s, lg_s, _ = _proj(xs, w, bf, wcg, bcg, tm_s, F32, 1.0)
        lf_s = lg_s[:, 0:N_HEADS].reshape(db, nq, N_HEADS)
        lf_col = jnp.pad(lf_s, ((0, 0), (0, 0), (0, LANE - N_HEADS)))
        lf_row = jnp.pad(jnp.transpose(lf_s, (0, 2, 1)), ((0, 0), (0, 8 - N_HEADS), (0, LANE - nq)))
        cab_s = _attn_sample(page_table, caches, l, qkv_s, kv_s, lf_row, lf_col, lam_params[l], gl, lam_init)
        pad_rows = lambda a: jnp.pad(a.reshape(db, nq, a.shape[-1]), ((0, 0), (0, BLK - nq), (0, 0)))
        ccd_s, sc_s, sd_s = _lin(pad_rows(cd_s), pad_rows(lg_s), state_c[l].reshape(db, LANE, HEAD_V),
                                 state_d[l].reshape(db, LANE, HEAD_V), gl, nq, F32)
        xs = post(xs, cab_s, ccd_s[:, 0:nq].reshape(db * nq, 512), tm=tm_s)
        heads_s = lambda a: a.reshape(db, nq, N_HEADS, HEAD_V)
        s_rows.append((heads_s(kv_s[:, 0:256]), heads_s(kv_s[:, 256:512]), heads_s(kv_s[:, 512:768]),
                       heads_s(kv_s[:, 768:1024]), lf_s,
                       sc_s.reshape(db, N_HEADS, LIN_DK, HEAD_V), sd_s.reshape(db, N_HEADS, LIN_DK, HEAD_V)))

    y_prompt = xp.reshape(nb, lp, D_MODEL)[:, N_META:l_true]
    y_sample = xs.reshape(db, nq, D_MODEL)
    ka_t, va_t, kb_t, vb_t, lf_t = row_outs
    heads_t = lambda a: jnp.transpose(a.reshape(depth, nb, N_HEADS, HEAD_V, l_true), (0, 1, 4, 2, 3))
    p_out = [heads_t(ka_t), heads_t(va_t), heads_t(kb_t), heads_t(vb_t), jnp.transpose(lf_t, (0, 1, 3, 2))]
    p_out += [jnp.stack(z) for z in zip(*p_states)]
    s_out = [jnp.stack(z) for z in zip(*s_rows)]
    return (y_prompt, y_sample, *p_out, *s_out)
```
